```python
import math
import jax, jax.numpy as jnp
from jax import lax
import numpy as np

D_MODEL = 1024
BATCH = 8
SEQ = 2048
DEPTH = 4
DEC_BATCH = 128
DEC_SEQ = 1
PAST_LEN = 2048
PAGE_SIZE = 128

N_A_LAYERS = DEPTH // 2
N_B_LAYERS = DEPTH - N_A_LAYERS
HEAD_DIM = 64
N_HEADS = D_MODEL // HEAD_DIM
ATTN_DIM = N_HEADS * HEAD_DIM
CONV_DIM = D_MODEL
CONV_WIDTH = 31
Q_BLOCK = 128
EPS = 1e-6

kernel_name = "yoco_conformer_conv_fox_decoder_step"


def rmsnorm(x, g):
    xf = x.astype(jnp.float32)
    y = xf * lax.rsqrt(jnp.mean(xf * xf, axis=-1, keepdims=True) + EPS)
    return (y * g.astype(jnp.float32)).astype(x.dtype)


def layernorm(x, g, b):
    xf = x.astype(jnp.float32)
    xc = xf - jnp.mean(xf, axis=-1, keepdims=True)
    y = xc * lax.rsqrt(jnp.mean(xc * xc, axis=-1, keepdims=True) + EPS)
    return (y * g.astype(jnp.float32) + b.astype(jnp.float32)).astype(x.dtype)


def depthwise_causal(u_full, w, b):
    y = lax.conv_general_dilated(
        u_full, w[:, None, :].astype(u_full.dtype), window_strides=(1,), padding='VALID',
        dimension_numbers=('NWC', 'WIO', 'NWC'), feature_group_count=u_full.shape[-1])
    return y + b.astype(y.dtype)


def conv_mixer(x, buf, norm_g, w_in, conv_w, conv_b, ln_g, ln_b, w_out):
    u = rmsnorm(x, norm_g) @ w_in
    a, g, z = jnp.split(u, 3, axis=-1)
    glu = a * jax.nn.sigmoid(g)
    full = jnp.concatenate([buf.astype(glu.dtype), glu], axis=1)
    y = jax.nn.silu(layernorm(depthwise_causal(full, conv_w, conv_b), ln_g, ln_b))
    out = (y * jax.nn.silu(z)) @ w_out
    new_buf = full[:, full.shape[1] - (CONV_WIDTH - 1):]
    return x + out, new_buf


def shared_kv(x, kv_norm, kv_w, kv_fb, k_norm):
    b, t, _ = x.shape
    u = rmsnorm(x, kv_norm) @ kv_w
    k = rmsnorm(u[..., :ATTN_DIM].reshape(b, t, N_HEADS, HEAD_DIM), k_norm)
    v = u[..., ATTN_DIM:2 * ATTN_DIM].reshape(b, t, N_HEADS, HEAD_DIM)
    logf = jax.nn.log_sigmoid((u[..., 2 * ATTN_DIM:] + kv_fb).astype(jnp.float32))
    return k, v, logf


def fox_attend(q, k, v, cq, ck, q_pos, k_pos):
    b, tq, h, d = q.shape
    blk = min(Q_BLOCK, tq)
    nb = tq // blk
    kf = k.astype(jnp.float32)
    vf = v.astype(jnp.float32)
    ckh = jnp.swapaxes(ck, 1, 2)
    qb = jnp.swapaxes(q.reshape(b, nb, blk, h, d), 0, 1)
    cqb = jnp.swapaxes(cq.reshape(b, nb, blk, h), 0, 1)
    pb = q_pos.reshape(nb, blk)
    scale = 1.0 / math.sqrt(d)

    def block(args):
        qi, ci, pi = args
        s = jnp.einsum('bqhd,bkhd->bhqk', qi.astype(jnp.float32), kf) * scale
        s = s + jnp.swapaxes(ci, 1, 2)[..., :, None] - ckh[..., None, :]
        s = jnp.where((k_pos[None, :] <= pi[:, None])[None, None], s, -jnp.inf)
        p = jax.nn.softmax(s, axis=-1)
        return jnp.einsum('bhqk,bkhd->bqhd', p, vf)

    o = lax.map(block, (qb, cqb, pb))
    return jnp.swapaxes(o, 0, 1).reshape(b, tq, h, d).astype(v.dtype)


def fox_mixer(x, k, v, cq, ck, q_pos, k_pos, norm_g, w_in, q_g, w_out):
    b, t, _ = x.shape
    u = rmsnorm(x, norm_g) @ w_in
    q = rmsnorm(u[..., :ATTN_DIM].reshape(b, t, N_HEADS, HEAD_DIM), q_g)
    z = u[..., ATTN_DIM:]
    o = fox_attend(q, k, v, cq, ck, q_pos, k_pos).reshape(b, t, ATTN_DIM)
    return x + (o * jax.nn.silu(z)) @ w_out


def setup_inputs(seed: int = 0) -> dict:
    key = jax.random.key(seed)
    ks = jax.random.split(key, 24)
    n_pages = PAST_LEN // PAGE_SIZE
    n_used = DEC_BATCH * n_pages
    n_pool = n_used + max(1, n_used // 4)
    nrm = jax.random.normal
    f32 = jnp.float32
    page_table = jax.random.permutation(ks[0], n_pool)[:n_used].reshape(DEC_BATCH, n_pages).astype(jnp.int32)
    return {
        'x_prompt': nrm(ks[1], (BATCH, SEQ, D_MODEL), f32),
        'x_sample': nrm(ks[2], (DEC_BATCH, DEC_SEQ, D_MODEL), f32),
        'state_conv': nrm(ks[3], (N_A_LAYERS, DEC_BATCH, CONV_WIDTH - 1, CONV_DIM), f32),
        'cache_k': nrm(ks[4], (n_pool, PAGE_SIZE, N_HEADS, HEAD_DIM), f32),
        'cache_v': nrm(ks[5], (n_pool, PAGE_SIZE, N_HEADS, HEAD_DIM), f32),
        'cache_logf': jax.nn.log_sigmoid(2.0 + 0.5 * nrm(ks[6], (n_pool, PAGE_SIZE, N_HEADS), f32)),
        'page_table': page_table,
        'a_norm': 1.0 + 0.02 * nrm(ks[7], (N_A_LAYERS, D_MODEL), f32),
        'a_w_in': nrm(ks[8], (N_A_LAYERS, D_MODEL, 3 * CONV_DIM), f32) * D_MODEL ** -0.5,
        'a_conv_w': nrm(ks[9], (N_A_LAYERS, CONV_WIDTH, CONV_DIM), f32) * CONV_WIDTH ** -0.5,
        'a_conv_b': 0.02 * nrm(ks[10], (N_A_LAYERS, CONV_DIM), f32),
        'a_ln_g': 1.0 + 0.02 * nrm(ks[11], (N_A_LAYERS, CONV_DIM), f32),
        'a_ln_b': 0.02 * nrm(ks[12], (N_A_LAYERS, CONV_DIM), f32),
        'a_w_out': nrm(ks[13], (N_A_LAYERS, CONV_DIM, D_MODEL), f32) * CONV_DIM ** -0.5,
        'kv_norm': 1.0 + 0.02 * nrm(ks[14], (D_MODEL,), f32),
        'kv_w': nrm(ks[15], (D_MODEL, 2 * ATTN_DIM + N_HEADS), f32) * D_MODEL ** -0.5,
        'kv_fb': 2.0 + 0.5 * nrm(ks[16], (N_HEADS,), f32),
        'k_norm': 1.0 + 0.02 * nrm(ks[17], (HEAD_DIM,), f32),
        'b_norm': 1.0 + 0.02 * nrm(ks[18], (N_B_LAYERS, D_MODEL), f32),
        'b_w_in': nrm(ks[19], (N_B_LAYERS, D_MODEL, 2 * ATTN_DIM), f32) * D_MODEL ** -0.5,
        'q_norm': 1.0 + 0.02 * nrm(ks[20], (N_B_LAYERS, HEAD_DIM), f32),
        'b_w_out': nrm(ks[21], (N_B_LAYERS, ATTN_DIM, D_MODEL), f32) * ATTN_DIM ** -0.5,
    }


def reference(x_prompt, x_sample, state_conv, cache_k, cache_v, cache_logf, page_table,
              a_norm, a_w_in, a_conv_w, a_conv_b, a_ln_g, a_ln_b, a_w_out,
              kv_norm, kv_w, kv_fb, k_norm, b_norm, b_w_in, q_norm, b_w_out):
    xp, xs = x_prompt, x_sample
    bp, tp, _ = xp.shape
    bs, ts, _ = xs.shape
    conv_p, conv_s = [], []
    for layer in range(DEPTH):
        if layer < N_A_LAYERS:
            i = layer
            buf0 = jnp.zeros((bp, CONV_WIDTH - 1, CONV_DIM), xp.dtype)
            xp, nbp = conv_mixer(xp, buf0, a_norm[i], a_w_in[i], a_conv_w[i], a_conv_b[i],
                                 a_ln_g[i], a_ln_b[i], a_w_out[i])
            xs, nbs = conv_mixer(xs, state_conv[i], a_norm[i], a_w_in[i], a_conv_w[i], a_conv_b[i],
                                 a_ln_g[i], a_ln_b[i], a_w_out[i])
            conv_p.append(nbp)
            conv_s.append(nbs)
            if layer == N_A_LAYERS - 1:
                k_p, v_p, logf_p = shared_kv(xp, kv_norm, kv_w, kv_fb, k_norm)
                k_s, v_s, logf_s = shared_kv(xs, kv_norm, kv_w, kv_fb, k_norm)
                c_p = jnp.cumsum(logf_p, axis=1)
                pos_p = jnp.arange(tp)
                past_k = cache_k[page_table].reshape(bs, -1, N_HEADS, HEAD_DIM)
                past_v = cache_v[page_table].reshape(bs, -1, N_HEADS, HEAD_DIM)
                past_f = cache_logf[page_table].reshape(bs, -1, N_HEADS)
                past_len = past_k.shape[1]
                k_all = jnp.concatenate([past_k.astype(k_s.dtype), k_s], axis=1)
                v_all = jnp.concatenate([past_v.astype(v_s.dtype), v_s], axis=1)
                c_all = jnp.cumsum(jnp.concatenate([past_f.astype(jnp.float32), logf_s], axis=1), axis=1)
                c_s = c_all[:, past_len:]
                qpos_s = past_len + jnp.arange(ts)
                kpos_s = jnp.arange(past_len + ts)
        else:
            j = layer - N_A_LAYERS
            xp = fox_mixer(xp, k_p, v_p, c_p, c_p, pos_p, pos_p,
                           b_norm[j], b_w_in[j], q_norm[j], b_w_out[j])
            xs = fox_mixer(xs, k_all, v_all, c_s, c_all, qpos_s, kpos_s,
                           b_norm[j], b_w_in[j], q_norm[j], b_w_out[j])
    conv_prompt = jnp.stack(conv_p, axis=0)
    conv_sample = jnp.stack(conv_s, axis=0)
    return (xp, xs, conv_prompt, conv_sample, k_p, v_p, logf_p, k_s, v_s, logf_s)
```

```python
import functools
import math

import jax
import jax.numpy as jnp
from jax import lax
from jax.experimental import pallas as pl
from jax.experimental.pallas import tpu as pltpu

F32 = jnp.float32
BF16 = jnp.bfloat16
EPS = 1e-6
HEAD_DIM = 64
CONV_WIDTH = 31
CONV_HALO = 32
SUBLANES = 8
LANES = 128
NEG_BIG = -1e30
V7X_VMEM_LIMIT_BYTES = 56 * 2**20

A_TILE = 512
CONV_ROWS = 64
KV_TILE = 256
Q_TILE = 256


def _rms_rows(x, g):
    return x * lax.rsqrt(jnp.mean(x * x, axis=-1, keepdims=True) + EPS) * g


def _silu(x):
    return x * jax.nn.sigmoid(x)


def _log_sigmoid(x):
    return -(jnp.maximum(-x, 0.0) + jnp.log1p(jnp.exp(-jnp.abs(x))))


def _layernorm_rows(x, g, b):
    xc = x - jnp.mean(x, axis=-1, keepdims=True)
    return xc * lax.rsqrt(jnp.mean(xc * xc, axis=-1, keepdims=True) + EPS) * g + b


def _head_rms_t(xt, g_col, n_heads):
    n = xt.shape[-1]
    x3 = xt.reshape(n_heads, HEAD_DIM, n)
    ms = jnp.mean(x3 * x3, axis=1, keepdims=True)
    return (x3 * lax.rsqrt(ms + EPS) * g_col[None]).reshape(n_heads * HEAD_DIM, n)


def _const_spec(shape):
    nd = len(shape)
    return pl.BlockSpec(shape, lambda *_: (0,) * nd, pipeline_mode=pl.Buffered(1))


def _params(*semantics):
    return pltpu.CompilerParams(dimension_semantics=semantics, vmem_limit_bytes=V7X_VMEM_LIMIT_BYTES)


def _conv_chunk(hist_ref, cw_ref, r0, rows, lsl):
    acc = None
    for phase in range(SUBLANES):
        offs = [o for o in range(2, CONV_HALO + 1) if o % SUBLANES == phase]
        n = rows if phase == 0 else rows + SUBLANES
        part = None
        for o in offs:
            term = cw_ref[o - 2:o - 1, lsl] * hist_ref[pl.ds(r0 + (o - phase), n), lsl]
            part = term if part is None else part + term
        part = part if phase == 0 else part[phase:phase + rows]
        acc = part if acc is None else acc + part
    return acc


def _a_prompt_body(x_ref, g_ref, win_ref, cw_ref, cb_ref, lg_ref, lb_ref, wout_ref,
                   o_ref, st_ref, hist_ref, sz_ref, yc_ref, *, tt):
    d = x_ref.shape[-1]
    t = pl.program_id(1)

    @pl.when(t == 0)
    def _():
        hist_ref[0:CONV_HALO, :] = jnp.zeros((CONV_HALO, d), F32)

    x = x_ref[0]
    xn = _rms_rows(x, g_ref[...]).astype(BF16)
    u = jnp.dot(xn, win_ref[...], preferred_element_type=F32)
    hist_ref[CONV_HALO:CONV_HALO + tt, :] = u[:, :d] * jax.nn.sigmoid(u[:, d:2 * d])
    sz_ref[...] = _silu(u[:, 2 * d:])

    def chunk(c, carry):
        r0 = pl.multiple_of(c * CONV_ROWS, CONV_ROWS)
        for l in range(d // LANES):
            lsl = slice(l * LANES, (l + 1) * LANES)
            yc_ref[pl.ds(r0, CONV_ROWS), lsl] = _conv_chunk(hist_ref, cw_ref, r0, CONV_ROWS, lsl)
        return carry

    lax.fori_loop(0, tt // CONV_ROWS, chunk, 0)

    y = _silu(_layernorm_rows(yc_ref[...] + cb_ref[...], lg_ref[...], lb_ref[...]))
    m = (y * sz_ref[...]).astype(BF16)
    o_ref[0] = x + jnp.dot(m, wout_ref[...], preferred_element_type=F32)

    @pl.when(t == pl.num_programs(1) - 1)
    def _():
        st_ref[0] = hist_ref[tt + CONV_HALO - (CONV_WIDTH - 1):tt + CONV_HALO, :]

    hist_ref[0:CONV_HALO, :] = hist_ref[tt:tt + CONV_HALO, :]


def _a_prompt(x, g, w_in, cw, cb, lg, lb, w_out):
    b, t, d = x.shape
    tt = A_TILE
    row = lambda: _const_spec((1, d))
    return pl.pallas_call(
        functools.partial(_a_prompt_body, tt=tt),
        grid=(b, t // tt),
        in_specs=[
            pl.BlockSpec((1, tt, d), lambda i, j: (i, j, 0)),
            row(), _const_spec((d, 3 * d)), _const_spec((CONV_WIDTH, d)), row(), row(), row(),
            _const_spec((d, d)),
        ],
        out_specs=[
            pl.BlockSpec((1, tt, d), lambda i, j: (i, j, 0)),
            pl.BlockSpec((1, CONV_WIDTH - 1, d), lambda i, j: (i, 0, 0)),
        ],
        out_shape=[
            jax.ShapeDtypeStruct((b, t, d), F32),
            jax.ShapeDtypeStruct((b, CONV_WIDTH - 1, d), F32),
        ],
        scratch_shapes=[
            pltpu.VMEM((tt + CONV_HALO, d), F32),
            pltpu.VMEM((tt, d), F32),
            pltpu.VMEM((tt, d), F32),
        ],
        compiler_params=_params("arbitrary", "arbitrary"),
        name="a_prompt",
    )(x, g, w_in, cw, cb, lg, lb, w_out)


def _kv_prompt_body(x_ref, g_ref, wk_ref, wv_ref, wft_ref, fb_ref, kg_ref, triu_ref, tril_ref,
                    kt_ref, vt_ref, lft_ref, kb_ref, vtb_ref, ct_ref, c_ref,
                    carry_t_ref, carry_n_ref, *, n_heads):
    t = pl.program_id(1)

    @pl.when(t == 0)
    def _():
        carry_t_ref[...] = jnp.zeros_like(carry_t_ref)
        carry_n_ref[...] = jnp.zeros_like(carry_n_ref)

    xn = _rms_rows(x_ref[0], g_ref[...]).astype(BF16)
    k = jnp.dot(xn, wk_ref[...], preferred_element_type=F32)
    v = jnp.dot(xn, wv_ref[...], preferred_element_type=F32)
    knt = _head_rms_t(k.T, kg_ref[...], n_heads)
    vt = v.T
    kt_ref[0] = knt
    vt_ref[0] = vt
    kb_ref[0] = knt.T.astype(BF16)
    vtb_ref[0, 0] = vt.astype(BF16)

    uft = lax.dot_general(wft_ref[...], xn, (((1,), (1,)), ((), ())), preferred_element_type=F32)
    lft = _log_sigmoid(uft + fb_ref[...])
    lft_ref[0] = lft
    ct = jnp.dot(lft, triu_ref[...], precision=lax.Precision.HIGHEST,
                 preferred_element_type=F32) + carry_t_ref[...]
    cn = lax.dot_general(tril_ref[...], lft, (((1,), (1,)), ((), ())), precision=lax.Precision.HIGHEST,
                         preferred_element_type=F32) + carry_n_ref[...]
    ct_ref[0] = ct
    c_ref[0] = cn
    tt = ct.shape[-1]
    carry_t_ref[...] = ct[:, tt - 1:tt]
    carry_n_ref[...] = cn[tt - 1:tt, :]


def _kv_prompt(x, g, wk, wv, wft, fb_col, kg_col):
    b, t, d = x.shape
    n_heads = wft.shape[0]
    tt = KV_TILE
    triu = jnp.triu(jnp.ones((tt, tt), F32))
    tril = jnp.tril(jnp.ones((tt, tt), F32))
    feat_major = lambda rows: pl.BlockSpec((1, rows, tt), lambda i, j: (i, 0, j))
    return pl.pallas_call(
        functools.partial(_kv_prompt_body, n_heads=n_heads),
        grid=(b, t // tt),
        in_specs=[
            pl.BlockSpec((1, tt, d), lambda i, j: (i, j, 0)),
            _const_spec((1, d)), _const_spec((d, d)), _const_spec((d, d)), _const_spec((n_heads, d)),
            _const_spec((n_heads, 1)), _const_spec((HEAD_DIM, 1)), _const_spec((tt, tt)), _const_spec((tt, tt)),
        ],
        out_specs=[
            feat_major(d), feat_major(d), feat_major(n_heads),
            pl.BlockSpec((1, tt, d), lambda i, j: (i, j, 0)),
            pl.BlockSpec((1, 1, d, tt), lambda i, j: (i, j, 0, 0)),
            feat_major(n_heads),
            pl.BlockSpec((1, tt, n_heads), lambda i, j: (i, j, 0)),
        ],
        out_shape=[
            jax.ShapeDtypeStruct((b, d, t), F32),
            jax.ShapeDtypeStruct((b, d, t), F32),
            jax.ShapeDtypeStruct((b, n_heads, t), F32),
            jax.ShapeDtypeStruct((b, t, d), BF16),
            jax.ShapeDtypeStruct((b, t // tt, d, tt), BF16),
            jax.ShapeDtypeStruct((b, n_heads, t), F32),
            jax.ShapeDtypeStruct((b, t, n_heads), F32),
        ],
        scratch_shapes=[pltpu.VMEM((n_heads, 1), F32), pltpu.VMEM((1, n_heads), F32)],
        compiler_params=_params("arbitrary", "arbitrary"),
        name="kv_prompt",
    )(x, g, wk, wv, wft, fb_col, kg_col, triu, tril)


def _b_prompt_body(x_ref, g_ref, win_ref, qg_ref, wout_ref, kb_ref, vtb_ref, ctq_ref, c_ref,
                   o_ref, qm_ref, ot_ref, sz_ref, *, n_heads, tq, tk):
    d = x_ref.shape[-1]
    i = pl.program_id(1)
    x = x_ref[0]
    xn = _rms_rows(x, g_ref[...]).astype(BF16)
    u = jnp.dot(xn, win_ref[...], preferred_element_type=F32)
    sz_ref[...] = _silu(u[:, d:])
    scale = 1.0 / math.sqrt(HEAD_DIM)
    qnt = (_head_rms_t(u[:, :d].T, qg_ref[...], n_heads) * scale).astype(BF16)
    zeros = jnp.zeros((HEAD_DIM, tq), BF16)
    for h in range(n_heads):
        qh = qnt[h * HEAD_DIM:(h + 1) * HEAD_DIM]
        lo, hi = (qh, zeros) if h % 2 == 0 else (zeros, qh)
        qm_ref[h, 0:HEAD_DIM, :] = lo
        qm_ref[h, HEAD_DIM:2 * HEAD_DIM, :] = hi

    key_idx = lax.broadcasted_iota(jnp.int32, (tk, tq), 0)
    qry_idx = lax.broadcasted_iota(jnp.int32, (tk, tq), 1)
    causal = key_idx <= qry_idx

    for h in range(n_heads):
        pair = slice((h // 2) * 2 * HEAD_DIM, (h // 2 + 1) * 2 * HEAD_DIM)
        cq = ctq_ref[0, h:h + 1, :]

        def step(j, carry, diagonal, h=h, pair=pair, cq=cq):
            m, l, acc = carry
            k0 = pl.multiple_of(j * tk, tk)
            st = jnp.dot(kb_ref[0, pl.ds(k0, tk), pair], qm_ref[h], preferred_element_type=F32)
            st = st + (cq - c_ref[0, pl.ds(k0, tk), h:h + 1])
            if diagonal:
                st = jnp.where(causal, st, NEG_BIG)
            m_new = jnp.maximum(m, jnp.max(st, axis=0, keepdims=True))
            alpha = jnp.exp(m - m_new)
            p = jnp.exp(st - m_new)
            l = alpha * l + jnp.sum(p, axis=0, keepdims=True)
            vh = vtb_ref[0, j, h * HEAD_DIM:(h + 1) * HEAD_DIM, :]
            acc = alpha * acc + jnp.dot(vh, p.astype(BF16), preferred_element_type=F32)
            return m_new, l, acc

        init = (jnp.full((1, tq), NEG_BIG, F32), jnp.zeros((1, tq), F32), jnp.zeros((HEAD_DIM, tq), F32))
        carry = lax.fori_loop(0, i, functools.partial(step, diagonal=False), init)
        _, l, acc = step(i, carry, True)
        ot_ref[h * HEAD_DIM:(h + 1) * HEAD_DIM, :] = acc / l

    m = (ot_ref[...].T * sz_ref[...]).astype(BF16)
    o_ref[0] = x + jnp.dot(m, wout_ref[...], preferred_element_type=F32)


def _b_prompt(x, g, w_in, qg_col, w_out, kb, vtb, ct, c):
    b, t, d = x.shape
    n_heads = ct.shape[1]
    tq, tk = Q_TILE, KV_TILE
    assert tq == tk and vtb.shape == (b, t // tk, d, tk)
    return pl.pallas_call(
        functools.partial(_b_prompt_body, n_heads=n_heads, tq=tq, tk=tk),
        grid=(b, t // tq),
        in_specs=[
            pl.BlockSpec((1, tq, d), lambda i, j: (i, j, 0)),
            _const_spec((1, d)), _const_spec((d, 2 * d)), _const_spec((HEAD_DIM, 1)), _const_spec((d, d)),
            pl.BlockSpec((1, t, d), lambda i, j: (i, 0, 0)),
            pl.BlockSpec((1, t // tk, d, tk), lambda i, j: (i, 0, 0, 0)),
            pl.BlockSpec((1, n_heads, tq), lambda i, j: (i, 0, j)),
            pl.BlockSpec((1, t, n_heads), lambda i, j: (i, 0, 0)),
        ],
        out_specs=pl.BlockSpec((1, tq, d), lambda i, j: (i, j, 0)),
        out_shape=jax.ShapeDtypeStruct((b, t, d), F32),
        scratch_shapes=[
            pltpu.VMEM((n_heads, 2 * HEAD_DIM, tq), BF16),
            pltpu.VMEM((d, tq), F32),
            pltpu.VMEM((tq, d), F32),
        ],
        compiler_params=_params("arbitrary", "arbitrary"),
        name="b_prompt",
    )(x, g, w_in, qg_col, w_out, kb, vtb, ct, c)


def _a_sample_body(x_ref, st_ref, cw_ref, g_ref, win_ref, cb_ref, lg_ref, lb_ref, wout_ref,
                   kvg_ref, wk_ref, wv_ref, wft_ref, fb_ref, kg_ref,
                   nst_ref, xo_ref, kst_ref, vst_ref, lfst_ref,
                   x_sc, glu_sc, sz_sc, acc_sc, *, n_heads):
    d = x_ref.shape[-1]
    layer = pl.program_id(0)
    s = pl.program_id(1)
    n_hist = CONV_WIDTH - 1

    @pl.when((layer == 0) & (s == 0))
    def _():
        x_sc[...] = x_ref[...]

    @pl.when(s == 0)
    def _():
        xn = _rms_rows(x_sc[...], g_ref[0]).astype(BF16)
        u = jnp.dot(xn, win_ref[0], preferred_element_type=F32)
        glu_sc[...] = u[:, :d] * jax.nn.sigmoid(u[:, d:2 * d])
        sz_sc[...] = _silu(u[:, 2 * d:])
        acc_sc[...] = jnp.zeros_like(acc_sc)

    @pl.when(s < n_hist)
    def _():
        acc_sc[...] += cw_ref[0, 0] * st_ref[0, 0]

    @pl.when((s >= 1) & (s < n_hist))
    def _():
        nst_ref[0, 0] = st_ref[0, 0]

    @pl.when(s == n_hist)
    def _():
        glu = glu_sc[...]
        nst_ref[0, 0] = glu
        yc = acc_sc[...] + cw_ref[0, 0] * glu + cb_ref[0]
        y = _silu(_layernorm_rows(yc, lg_ref[0], lb_ref[0]))
        m = (y * sz_sc[...]).astype(BF16)
        x_sc[...] = x_sc[...] + jnp.dot(m, wout_ref[0], preferred_element_type=F32)

    @pl.when((s == n_hist) & (layer == pl.num_programs(0) - 1))
    def _():
        x = x_sc[...]
        xo_ref[...] = x
        xn = _rms_rows(x, kvg_ref[...]).astype(BF16)
        k = jnp.dot(xn, wk_ref[...], preferred_element_type=F32)
        v = jnp.dot(xn, wv_ref[...], preferred_element_type=F32)
        kst_ref[...] = _head_rms_t(k.T, kg_ref[...], n_heads)
        vst_ref[...] = v.T
        uft = lax.dot_general(wft_ref[...], xn, (((1,), (1,)), ((), ())), preferred_element_type=F32)
        lfst_ref[...] = _log_sigmoid(uft + fb_ref[...])


def _a_sample(x, st, cw4, g, w_in, cb, lg, lb, w_out, kvg, wk, wv, wft, fb_col, kg_col):
    n, d = x.shape
    n_layers, n_hist = st.shape[0], st.shape[1]
    n_heads = wft.shape[0]
    per_layer = lambda *shape: pl.BlockSpec((1,) + shape, lambda l, s: (l,) + (0,) * len(shape))
    return pl.pallas_call(
        functools.partial(_a_sample_body, n_heads=n_heads),
        grid=(n_layers, n_hist + 1),
        in_specs=[
            _const_spec((n, d)),
            pl.BlockSpec((1, 1, n, d), lambda l, s: (l, jnp.minimum(s, n_hist - 1), 0, 0)),
            pl.BlockSpec((1, 1, 1, d), lambda l, s: (l, s, 0, 0)),
            per_layer(1, d), per_layer(d, 3 * d), per_layer(1, d), per_layer(1, d), per_layer(1, d),
            per_layer(d, d),
            _const_spec((1, d)), _const_spec((d, d)), _const_spec((d, d)), _const_spec((n_heads, d)),
            _const_spec((n_heads, 1)), _const_spec((HEAD_DIM, 1)),
        ],
        out_specs=[
            pl.BlockSpec((1, 1, n, d), lambda l, s: (l, jnp.maximum(s - 1, 0), 0, 0)),
            pl.BlockSpec((n, d), lambda l, s: (0, 0)),
            pl.BlockSpec((d, n), lambda l, s: (0, 0)),
            pl.BlockSpec((d, n), lambda l, s: (0, 0)),
            pl.BlockSpec((n_heads, n), lambda l, s: (0, 0)),
        ],
        out_shape=[
            jax.ShapeDtypeStruct((n_layers, n_hist, n, d), F32),
            jax.ShapeDtypeStruct((n, d), F32),
            jax.ShapeDtypeStruct((d, n), F32),
            jax.ShapeDtypeStruct((d, n), F32),
            jax.ShapeDtypeStruct((n_heads, n), F32),
        ],
        scratch_shapes=[pltpu.VMEM((n, d), F32)] * 4,
        compiler_params=_params("arbitrary", "arbitrary"),
        name="a_sample",
    )(x, st, cw4, g, w_in, cb, lg, lb, w_out, kvg, wk, wv, wft, fb_col, kg_col)


def _bias_sample_body(pt_ref, lfc_ref, lfst_ref, lower_ref, o_ref, *, n_pages):
    b = pl.program_id(0)
    pages = [lfc_ref[pt_ref[b, j]] for j in range(n_pages)]
    page = pages[0].shape[-1]
    n_heads = pages[0].shape[0]
    within = jnp.dot(jnp.concatenate(pages, axis=0), lower_ref[...], precision=lax.Precision.HIGHEST,
                     preferred_element_type=F32)
    lane = lax.broadcasted_iota(jnp.int32, lfst_ref.shape, 1)
    carry = jnp.sum(jnp.where(lane == b, lfst_ref[...], 0.0), axis=1, keepdims=True)
    for j in reversed(range(n_pages)):
        o_ref[0, :, j * page:(j + 1) * page] = within[j * n_heads:(j + 1) * n_heads] + carry
        carry = carry + jnp.sum(pages[j], axis=1, keepdims=True)


def _bias_sample(page_table, lfc, lfst):
    n, n_pages = page_table.shape
    n_pool, n_heads, page = lfc.shape
    lower = jnp.tril(jnp.ones((page, page), F32), k=-1)
    return pl.pallas_call(
        functools.partial(_bias_sample_body, n_pages=n_pages),
        grid_spec=pltpu.PrefetchScalarGridSpec(
            num_scalar_prefetch=1,
            grid=(n,),
            in_specs=[_const_spec((n_pool, n_heads, page)), _const_spec((n_heads, n)), _const_spec((page, page))],
            out_specs=pl.BlockSpec((1, n_heads, n_pages * page), lambda i, pt: (i, 0, 0)),
        ),
        out_shape=jax.ShapeDtypeStruct((n, n_heads, n_pages * page), F32),
        compiler_params=_params("arbitrary"),
        name="bias_sample",
    )(page_table, lfc, lfst, lower)


def _b_sample_body(pt_ref, x_ref, g_ref, win_ref, qg_ref, wout_ref, bias_ref, kst_ref, vst_ref, *rest,
                   n_heads, n_pages):
    k_refs = rest[:n_pages]
    v_refs = rest[n_pages:2 * n_pages]
    o_ref, q_sc, sz_sc, o_sc = rest[2 * n_pages:]
    n, d = x_ref.shape
    b = pl.program_id(0)

    @pl.when(b == 0)
    def _():
        xn = _rms_rows(x_ref[...], g_ref[...]).astype(BF16)
        u = jnp.dot(xn, win_ref[...], preferred_element_type=F32)
        scale = 1.0 / math.sqrt(HEAD_DIM)
        q_sc[...] = (_head_rms_t(u[:, :d].T, qg_ref[...], n_heads) * scale).T
        sz_sc[...] = _silu(u[:, d:])

    own = (lax.broadcasted_iota(jnp.int32, (n_heads, d), 1) // HEAD_DIM
           == lax.broadcasted_iota(jnp.int32, (n_heads, d), 0))
    qbd = jnp.where(own, jnp.broadcast_to(q_sc[pl.ds(b, 1), :], (n_heads, d)), 0.0).astype(BF16)

    s = jnp.concatenate(
        [jnp.dot(qbd, k_refs[j][0].astype(BF16), preferred_element_type=F32) for j in range(n_pages)],
        axis=1) + bias_ref[0]
    is_b = lax.broadcasted_iota(jnp.int32, (n_heads, n), 1) == b
    k_self = jnp.where(lax.broadcasted_iota(jnp.int32, (d, n), 1) == b, kst_ref[...], 0.0).astype(BF16)
    s_self = jnp.where(is_b, jnp.dot(qbd, k_self, preferred_element_type=F32), NEG_BIG)

    m = jnp.maximum(jnp.max(s, axis=1, keepdims=True), jnp.max(s_self, axis=1, keepdims=True))
    p = jnp.exp(s - m)
    p_self = jnp.exp(s_self - m)
    l = jnp.sum(p, axis=1, keepdims=True) + jnp.sum(p_self, axis=1, keepdims=True)
    pb = p.astype(BF16)
    page = s_self.shape[-1]
    nt = (((1,), (1,)), ((), ()))
    o = lax.dot_general(p_self.astype(BF16), vst_ref[...].astype(BF16), nt, preferred_element_type=F32)
    for j in range(n_pages):
        o = o + lax.dot_general(pb[:, j * page:(j + 1) * page], v_refs[j][0].astype(BF16), nt,
                                preferred_element_type=F32)
    o_sc[pl.ds(b, 1), :] = jnp.sum(jnp.where(own, o / l, 0.0), axis=0, keepdims=True)

    @pl.when(b == pl.num_programs(0) - 1)
    def _():
        mm = (o_sc[...] * sz_sc[...]).astype(BF16)
        o_ref[...] = x_ref[...] + jnp.dot(mm, wout_ref[...], preferred_element_type=F32)


def _b_sample(page_table, x, g, w_in, qg_col, w_out, bias, kst, vst, ckt, cvt):
    n, d = x.shape
    n_pages = page_table.shape[1]
    n_heads = bias.shape[1]
    page = ckt.shape[-1]
    assert page == n, "the new token's key rides through the kernel as one more page-sized block"
    page_spec = lambda j: pl.BlockSpec((1, d, page), lambda i, pt, j=j: (pt[i, j], 0, 0))
    return pl.pallas_call(
        functools.partial(_b_sample_body, n_heads=n_heads, n_pages=n_pages),
        grid_spec=pltpu.PrefetchScalarGridSpec(
            num_scalar_prefetch=1,
            grid=(n,),
            in_specs=[
                _const_spec((n, d)), _const_spec((1, d)), _const_spec((d, 2 * d)), _const_spec((HEAD_DIM, 1)),
                _const_spec((d, d)),
                pl.BlockSpec((1, n_heads, n_pages * page), lambda i, pt: (i, 0, 0)),
                _const_spec((d, n)), _const_spec((d, n)),
            ] + [page_spec(j) for j in range(n_pages)] * 2,
            out_specs=pl.BlockSpec((n, d), lambda i, pt: (0, 0)),
            scratch_shapes=[pltpu.VMEM((n, d), F32)] * 3,
        ),
        out_shape=jax.ShapeDtypeStruct((n, d), F32),
        compiler_params=_params("arbitrary"),
        name="b_sample",
    )(page_table, x, g, w_in, qg_col, w_out, bias, kst, vst, *([ckt] * n_pages), *([cvt] * n_pages))


def kernel(x_prompt, x_sample, state_conv, cache_k, cache_v, cache_logf, page_table, a_norm, a_w_in, a_conv_w,
           a_conv_b, a_ln_g, a_ln_b, a_w_out, kv_norm, kv_w, kv_fb, k_norm, b_norm, b_w_in, q_norm, b_w_out):
    bp, tp, d = x_prompt.shape
    bs = x_sample.shape[0]
    n_a = a_w_in.shape[0]
    n_b = b_w_in.shape[0]
    n_heads = kv_fb.shape[0]
    attn = n_heads * HEAD_DIM
    n_pool, page = cache_k.shape[0], cache_k.shape[1]

    a_w_in_b = a_w_in.astype(BF16)
    a_w_out_b = a_w_out.astype(BF16)
    b_w_in_b = b_w_in.astype(BF16)
    b_w_out_b = b_w_out.astype(BF16)
    wk = kv_w[:, :attn].astype(BF16)
    wv = kv_w[:, attn:2 * attn].astype(BF16)
    wft = kv_w[:, 2 * attn:].T.astype(BF16)
    fb_col = kv_fb.reshape(n_heads, 1)
    kg_col = k_norm.reshape(HEAD_DIM, 1)
    kvg = kv_norm.reshape(1, d)

    xp = x_prompt
    conv_p = []
    for i in range(n_a):
        xp, st = _a_prompt(xp, a_norm[i][None], a_w_in_b[i], a_conv_w[i], a_conv_b[i][None], a_ln_g[i][None],
                           a_ln_b[i][None], a_w_out_b[i])
        conv_p.append(st)
    kt, vt, lft, kb, vtb, ct, c = _kv_prompt(xp, kvg, wk, wv, wft, fb_col, kg_col)
    for j in range(n_b):
        xp = _b_prompt(xp, b_norm[j][None], b_w_in_b[j], q_norm[j].reshape(HEAD_DIM, 1), b_w_out_b[j],
                       kb, vtb, ct, c)

    st_t = jnp.transpose(state_conv, (0, 2, 1, 3))
    nst_t, xs, kst, vst, lfst = _a_sample(
        x_sample.reshape(bs, d), st_t, a_conv_w[:, :, None, :], a_norm[:, None], a_w_in_b, a_conv_b[:, None],
        a_ln_g[:, None], a_ln_b[:, None], a_w_out_b, kvg, wk, wv, wft, fb_col, kg_col)
    ckt = jnp.transpose(cache_k, (0, 2, 3, 1)).reshape(n_pool, attn, page)
    cvt = jnp.transpose(cache_v, (0, 2, 3, 1)).reshape(n_pool, attn, page)
    lfc = jnp.transpose(cache_logf, (0, 2, 1))
    bias = _bias_sample(page_table, lfc, lfst)
    for j in range(n_b):
        xs = _b_sample(page_table, xs, b_norm[j][None], b_w_in_b[j], q_norm[j].reshape(HEAD_DIM, 1),
                       b_w_out_b[j], bias, kst, vst, ckt, cvt)

    heads_last = lambda a: jnp.transpose(a.reshape(a.shape[0], n_heads, HEAD_DIM, a.shape[-1]), (0, 3, 1, 2))
    return (
        xp,
        xs.reshape(bs, 1, d),
        jnp.stack(conv_p, axis=0),
        jnp.transpose(nst_t, (0, 2, 1, 3)),
        heads_last(kt),
        heads_last(vt),
        jnp.transpose(lft, (0, 2, 1)),
        jnp.transpose(kst.reshape(n_heads, HEAD_DIM, bs), (2, 0, 1))[:, None],
        jnp.transpose(vst.reshape(n_heads, HEAD_DIM, bs), (2, 0, 1))[:, None],
        jnp.transpose(lfst, (1, 0))[:, None],
    )
```

```python
import functools
import math

import jax
import jax.numpy as jnp
from jax import lax
from jax.experimental import pallas as pl
from jax.experimental.pallas import tpu as pltpu

F32 = jnp.float32
BF16 = jnp.bfloat16
EPS = 1e-6
HEAD_DIM = 64
CONV_WIDTH = 31
CONV_HALO = 32
SUBLANES = 8
LANES = 128
NEG_BIG = -1e30
V7X_VMEM_LIMIT_BYTES = 56 * 2**20

A_TILE = 512
CONV_ROWS = 64
KV_TILE = 256
Q_TILE = 256
LOG2E = 1.4426950408889634
N_SPLIT = 3


def _rms_rows(x, g):
    return x * lax.rsqrt(jnp.mean(x * x, axis=-1, keepdims=True) + EPS) * g


def _silu(x):
    return x * jax.nn.sigmoid(x)


def _log_sigmoid(x):
    return -(jnp.maximum(-x, 0.0) + jnp.log1p(jnp.exp(-jnp.abs(x))))


def _layernorm_rows(x, g, b):
    xc = x - jnp.mean(x, axis=-1, keepdims=True)
    return xc * lax.rsqrt(jnp.mean(xc * xc, axis=-1, keepdims=True) + EPS) * g + b


def _head_rms_t(xt, g_col, n_heads):
    n = xt.shape[-1]
    x3 = xt.reshape(n_heads, HEAD_DIM, n)
    ms = jnp.mean(x3 * x3, axis=1, keepdims=True)
    return (x3 * lax.rsqrt(ms + EPS) * g_col[None]).reshape(n_heads * HEAD_DIM, n)


def _const_spec(shape):
    nd = len(shape)
    return pl.BlockSpec(shape, lambda *_: (0,) * nd, pipeline_mode=pl.Buffered(1))


def _params(*semantics):
    return pltpu.CompilerParams(dimension_semantics=semantics, vmem_limit_bytes=V7X_VMEM_LIMIT_BYTES)


def _conv_chunk(hist_ref, cw_ref, r0, rows, lsl):
    acc = None
    for phase in range(SUBLANES):
        offs = [o for o in range(2, CONV_HALO + 1) if o % SUBLANES == phase]
        n = rows if phase == 0 else rows + SUBLANES
        part = None
        for o in offs:
            term = cw_ref[o - 2:o - 1, lsl] * hist_ref[pl.ds(r0 + (o - phase), n), lsl]
            part = term if part is None else part + term
        part = part if phase == 0 else part[phase:phase + rows]
        acc = part if acc is None else acc + part
    return acc


def _a_prompt_body(x_ref, g_ref, win_ref, cw_ref, cb_ref, lg_ref, lb_ref, wout_ref,
                   o_ref, st_ref, hist_ref, sz_ref, yc_ref, *, tt):
    d = x_ref.shape[-1]
    t = pl.program_id(1)

    @pl.when(t == 0)
    def _():
        hist_ref[0:CONV_HALO, :] = jnp.zeros((CONV_HALO, d), F32)

    x = x_ref[0]
    xn = _rms_rows(x, g_ref[...]).astype(BF16)
    u = jnp.dot(xn, win_ref[...], preferred_element_type=F32)
    hist_ref[CONV_HALO:CONV_HALO + tt, :] = u[:, :d] * jax.nn.sigmoid(u[:, d:2 * d])
    sz_ref[...] = _silu(u[:, 2 * d:])

    def chunk(c, carry):
        r0 = pl.multiple_of(c * CONV_ROWS, CONV_ROWS)
        for l in range(d // LANES):
            lsl = slice(l * LANES, (l + 1) * LANES)
            yc_ref[pl.ds(r0, CONV_ROWS), lsl] = _conv_chunk(hist_ref, cw_ref, r0, CONV_ROWS, lsl)
        return carry

    lax.fori_loop(0, tt // CONV_ROWS, chunk, 0)

    y = _silu(_layernorm_rows(yc_ref[...] + cb_ref[...], lg_ref[...], lb_ref[...]))
    m = (y * sz_ref[...]).astype(BF16)
    o_ref[0] = x + jnp.dot(m, wout_ref[...], preferred_element_type=F32)

    @pl.when(t == pl.num_programs(1) - 1)
    def _():
        st_ref[0] = hist_ref[tt + CONV_HALO - (CONV_WIDTH - 1):tt + CONV_HALO, :]

    hist_ref[0:CONV_HALO, :] = hist_ref[tt:tt + CONV_HALO, :]


def _a_prompt(x, g, w_in, cw, cb, lg, lb, w_out):
    b, t, d = x.shape
    tt = A_TILE
    row = lambda: _const_spec((1, d))
    return pl.pallas_call(
        functools.partial(_a_prompt_body, tt=tt),
        grid=(b, t // tt),
        in_specs=[
            pl.BlockSpec((1, tt, d), lambda i, j: (i, j, 0)),
            row(), _const_spec((d, 3 * d)), _const_spec((CONV_WIDTH, d)), row(), row(), row(),
            _const_spec((d, d)),
        ],
        out_specs=[
            pl.BlockSpec((1, tt, d), lambda i, j: (i, j, 0)),
            pl.BlockSpec((1, CONV_WIDTH - 1, d), lambda i, j: (i, 0, 0)),
        ],
        out_shape=[
            jax.ShapeDtypeStruct((b, t, d), F32),
            jax.ShapeDtypeStruct((b, CONV_WIDTH - 1, d), F32),
        ],
        scratch_shapes=[
            pltpu.VMEM((tt + CONV_HALO, d), F32),
            pltpu.VMEM((tt, d), F32),
            pltpu.VMEM((tt, d), F32),
        ],
        compiler_params=_params("arbitrary", "arbitrary"),
        name="a_prompt",
    )(x, g, w_in, cw, cb, lg, lb, w_out)


def _split_bf16(x):
    parts = []
    for _ in range(N_SPLIT - 1):
        hi = x.astype(BF16).astype(F32)
        parts.append(hi)
        x = x - hi
    return parts + [x]


def _kv_prompt_body(x_ref, g_ref, wk_ref, wv_ref, wft_ref, fb_ref, kg_ref, triu_ref, tril_ref,
                    kt_ref, vt_ref, lft_ref, kb_ref, vtb_ref, ct_ref,
                    carry_t_ref, carry_n_ref, *, n_heads):
    t = pl.program_id(1)

    @pl.when(t == 0)
    def _():
        carry_t_ref[...] = jnp.zeros_like(carry_t_ref)
        carry_n_ref[...] = jnp.zeros_like(carry_n_ref)

    xn = _rms_rows(x_ref[0], g_ref[...]).astype(BF16)
    tt = xn.shape[0]
    k = jnp.dot(xn, wk_ref[...], preferred_element_type=F32)
    v = jnp.dot(xn, wv_ref[...], preferred_element_type=F32)
    knt = _head_rms_t(k.T, kg_ref[...], n_heads)
    vt = v.T
    kt_ref[0] = knt
    vt_ref[0] = vt
    vtb_ref[0, 0] = vt.astype(BF16)

    uft = lax.dot_general(wft_ref[...], xn, (((1,), (1,)), ((), ())), preferred_element_type=F32)
    lft = _log_sigmoid(uft + fb_ref[...])
    lft_ref[0] = lft
    ct = jnp.dot(lft, triu_ref[...], precision=lax.Precision.HIGHEST,
                 preferred_element_type=F32) + carry_t_ref[...]
    cn = lax.dot_general(tril_ref[...], lft, (((1,), (1,)), ((), ())), precision=lax.Precision.HIGHEST,
                         preferred_element_type=F32) + carry_n_ref[...]
    ct_ref[0] = ct
    carry_t_ref[...] = ct[:, tt - 1:tt]
    carry_n_ref[...] = cn[tt - 1:tt, :]

    kn = knt.T
    lane = lax.broadcasted_iota(jnp.int32, (tt, HEAD_DIM), 1)
    for h in range(n_heads):
        terms = _split_bf16(cn[:, h:h + 1] * (-LOG2E))
        aug = jnp.zeros((tt, HEAD_DIM), F32)
        for i, term in enumerate(terms):
            aug = jnp.where(lane == i, term, aug)
        blk = jnp.concatenate([kn[:, h * HEAD_DIM:(h + 1) * HEAD_DIM], aug], axis=1)
        kb_ref[0, :, h * 2 * HEAD_DIM:(h + 1) * 2 * HEAD_DIM] = blk.astype(BF16)


def _kv_prompt(x, g, wk, wv, wft, fb_col, kg_col):
    b, t, d = x.shape
    n_heads = wft.shape[0]
    tt = KV_TILE
    triu = jnp.triu(jnp.ones((tt, tt), F32))
    tril = jnp.tril(jnp.ones((tt, tt), F32))
    feat_major = lambda rows: pl.BlockSpec((1, rows, tt), lambda i, j: (i, 0, j))
    return pl.pallas_call(
        functools.partial(_kv_prompt_body, n_heads=n_heads),
        grid=(b, t // tt),
        in_specs=[
            pl.BlockSpec((1, tt, d), lambda i, j: (i, j, 0)),
            _const_spec((1, d)), _const_spec((d, d)), _const_spec((d, d)), _const_spec((n_heads, d)),
            _const_spec((n_heads, 1)), _const_spec((HEAD_DIM, 1)), _const_spec((tt, tt)), _const_spec((tt, tt)),
        ],
        out_specs=[
            feat_major(d), feat_major(d), feat_major(n_heads),
            pl.BlockSpec((1, tt, 2 * d), lambda i, j: (i, j, 0)),
            pl.BlockSpec((1, 1, d, tt), lambda i, j: (i, j, 0, 0)),
            feat_major(n_heads),
        ],
        out_shape=[
            jax.ShapeDtypeStruct((b, d, t), F32),
            jax.ShapeDtypeStruct((b, d, t), F32),
            jax.ShapeDtypeStruct((b, n_heads, t), F32),
            jax.ShapeDtypeStruct((b, t, 2 * d), BF16),
            jax.ShapeDtypeStruct((b, t // tt, d, tt), BF16),
            jax.ShapeDtypeStruct((b, n_heads, t), F32),
        ],
        scratch_shapes=[pltpu.VMEM((n_heads, 1), F32), pltpu.VMEM((1, n_heads), F32)],
        compiler_params=_params("arbitrary", "arbitrary"),
        name="kv_prompt",
    )(x, g, wk, wv, wft, fb_col, kg_col, triu, tril)


def _b_prompt_body(x_ref, g_ref, win_ref, qg_ref, wout_ref, kb_ref, vtb_ref, ctq_ref,
                   o_ref, qm_ref, ot_ref, sz_ref, m_ref, l_ref, acc_ref, *, n_heads, tq, tk):
    d = x_ref.shape[-1]
    i = pl.program_id(1)
    x = x_ref[0]
    xn = _rms_rows(x, g_ref[...]).astype(BF16)
    u = jnp.dot(xn, win_ref[...], preferred_element_type=F32)
    sz_ref[...] = _silu(u[:, d:])
    scale = LOG2E / math.sqrt(HEAD_DIM)
    qnt = (_head_rms_t(u[:, :d].T, qg_ref[...], n_heads) * scale).astype(BF16)
    ones = (lax.broadcasted_iota(jnp.int32, (HEAD_DIM, tq), 0) < N_SPLIT).astype(BF16)
    for h in range(n_heads):
        qm_ref[h, 0:HEAD_DIM, :] = qnt[h * HEAD_DIM:(h + 1) * HEAD_DIM]
        qm_ref[h, HEAD_DIM:2 * HEAD_DIM, :] = ones
        m_ref[h] = jnp.full((1, tq), NEG_BIG, F32)
        l_ref[h] = jnp.zeros((1, tq), F32)
        acc_ref[h] = jnp.zeros((HEAD_DIM, tq), F32)

    causal = (lax.broadcasted_iota(jnp.int32, (tk, tq), 0) <= lax.broadcasted_iota(jnp.int32, (tk, tq), 1))

    def step(j, carry, diagonal):
        k0 = pl.multiple_of(j * tk, tk)
        sts = [jnp.dot(kb_ref[0, pl.ds(k0, tk), h * 2 * HEAD_DIM:(h + 1) * 2 * HEAD_DIM], qm_ref[h],
                       preferred_element_type=F32) for h in range(n_heads)]
        ps, alphas = [], []
        for h, st in enumerate(sts):
            cq = ctq_ref[0, h:h + 1, :] * LOG2E
            if diagonal:
                st = jnp.where(causal, st, NEG_BIG)
            m = m_ref[h]
            m_new = jnp.maximum(m, jnp.max(st, axis=0, keepdims=True) + cq)
            alpha = jnp.exp2(m - m_new)
            p = jnp.exp2(st - (m_new - cq))
            m_ref[h] = m_new
            l_ref[h] = alpha * l_ref[h] + jnp.sum(p, axis=0, keepdims=True)
            ps.append(p.astype(BF16))
            alphas.append(alpha)
        for h, (p, alpha) in enumerate(zip(ps, alphas)):
            vh = vtb_ref[0, j, h * HEAD_DIM:(h + 1) * HEAD_DIM, :]
            acc_ref[h] = alpha * acc_ref[h] + jnp.dot(vh, p, preferred_element_type=F32)
        return carry

    lax.fori_loop(0, i, functools.partial(step, diagonal=False), 0)
    step(i, 0, True)
    for h in range(n_heads):
        ot_ref[h * HEAD_DIM:(h + 1) * HEAD_DIM, :] = acc_ref[h] / l_ref[h]

    m = (ot_ref[...].T * sz_ref[...]).astype(BF16)
    o_ref[0] = x + jnp.dot(m, wout_ref[...], preferred_element_type=F32)


def _b_prompt(x, g, w_in, qg_col, w_out, kb, vtb, ct):
    b, t, d = x.shape
    n_heads = ct.shape[1]
    tq, tk = Q_TILE, KV_TILE
    assert tq == tk and vtb.shape == (b, t // tk, d, tk)
    return pl.pallas_call(
        functools.partial(_b_prompt_body, n_heads=n_heads, tq=tq, tk=tk),
        grid=(b, t // tq),
        in_specs=[
            pl.BlockSpec((1, tq, d), lambda i, j: (i, j, 0)),
            _const_spec((1, d)), _const_spec((d, 2 * d)), _const_spec((HEAD_DIM, 1)), _const_spec((d, d)),
            pl.BlockSpec((1, t, 2 * d), lambda i, j: (i, 0, 0)),
            pl.BlockSpec((1, t // tk, d, tk), lambda i, j: (i, 0, 0, 0)),
            pl.BlockSpec((1, n_heads, tq), lambda i, j: (i, 0, j)),
        ],
        out_specs=pl.BlockSpec((1, tq, d), lambda i, j: (i, j, 0)),
        out_shape=jax.ShapeDtypeStruct((b, t, d), F32),
        scratch_shapes=[
            pltpu.VMEM((n_heads, 2 * HEAD_DIM, tq), BF16),
            pltpu.VMEM((d, tq), F32),
            pltpu.VMEM((tq, d), F32),
            pltpu.VMEM((n_heads, 1, tq), F32),
            pltpu.VMEM((n_heads, 1, tq), F32),
            pltpu.VMEM((n_heads, HEAD_DIM, tq), F32),
        ],
        compiler_params=_params("arbitrary", "arbitrary"),
        name="b_prompt",
    )(x, g, w_in, qg_col, w_out, kb, vtb, ct)


def _a_sample_body(x_ref, st_ref, cw_ref, g_ref, win_ref, cb_ref, lg_ref, lb_ref, wout_ref,
                   kvg_ref, wk_ref, wv_ref, wft_ref, fb_ref, kg_ref,
                   nst_ref, xo_ref, kst_ref, vst_ref, lfst_ref,
                   x_sc, glu_sc, sz_sc, acc_sc, *, n_heads):
    d = x_ref.shape[-1]
    layer = pl.program_id(0)
    s = pl.program_id(1)
    n_hist = CONV_WIDTH - 1

    @pl.when((layer == 0) & (s == 0))
    def _():
        x_sc[...] = x_ref[...]

    @pl.when(s == 0)
    def _():
        xn = _rms_rows(x_sc[...], g_ref[0]).astype(BF16)
        u = jnp.dot(xn, win_ref[0], preferred_element_type=F32)
        glu_sc[...] = u[:, :d] * jax.nn.sigmoid(u[:, d:2 * d])
        sz_sc[...] = _silu(u[:, 2 * d:])
        acc_sc[...] = jnp.zeros_like(acc_sc)

    @pl.when(s < n_hist)
    def _():
        acc_sc[...] += cw_ref[0, 0] * st_ref[0, 0]

    @pl.when((s >= 1) & (s < n_hist))
    def _():
        nst_ref[0, 0] = st_ref[0, 0]

    @pl.when(s == n_hist)
    def _():
        glu = glu_sc[...]
        nst_ref[0, 0] = glu
        yc = acc_sc[...] + cw_ref[0, 0] * glu + cb_ref[0]
        y = _silu(_layernorm_rows(yc, lg_ref[0], lb_ref[0]))
        m = (y * sz_sc[...]).astype(BF16)
        x_sc[...] = x_sc[...] + jnp.dot(m, wout_ref[0], preferred_element_type=F32)

    @pl.when((s == n_hist) & (layer == pl.num_programs(0) - 1))
    def _():
        x = x_sc[...]
        xo_ref[...] = x
        xn = _rms_rows(x, kvg_ref[...]).astype(BF16)
        k = jnp.dot(xn, wk_ref[...], preferred_element_type=F32)
        v = jnp.dot(xn, wv_ref[...], preferred_element_type=F32)
        kst_ref[...] = _head_rms_t(k.T, kg_ref[...], n_heads)
        vst_ref[...] = v.T
        uft = lax.dot_general(wft_ref[...], xn, (((1,), (1,)), ((), ())), preferred_element_type=F32)
        lfst_ref[...] = _log_sigmoid(uft + fb_ref[...])


def _a_sample(x, st, cw4, g, w_in, cb, lg, lb, w_out, kvg, wk, wv, wft, fb_col, kg_col):
    n, d = x.shape
    n_layers, n_hist = st.shape[0], st.shape[1]
    n_heads = wft.shape[0]
    per_layer = lambda *shape: pl.BlockSpec((1,) + shape, lambda l, s: (l,) + (0,) * len(shape))
    return pl.pallas_call(
        functools.partial(_a_sample_body, n_heads=n_heads),
        grid=(n_layers, n_hist + 1),
        in_specs=[
            _const_spec((n, d)),
            pl.BlockSpec((1, 1, n, d), lambda l, s: (l, jnp.minimum(s, n_hist - 1), 0, 0)),
            pl.BlockSpec((1, 1, 1, d), lambda l, s: (l, s, 0, 0)),
            per_layer(1, d), per_layer(d, 3 * d), per_layer(1, d), per_layer(1, d), per_layer(1, d),
            per_layer(d, d),
            _const_spec((1, d)), _const_spec((d, d)), _const_spec((d, d)), _const_spec((n_heads, d)),
            _const_spec((n_heads, 1)), _const_spec((HEAD_DIM, 1)),
        ],
        out_specs=[
            pl.BlockSpec((1, 1, n, d), lambda l, s: (l, jnp.maximum(s - 1, 0), 0, 0)),
            pl.BlockSpec((n, d), lambda l, s: (0, 0)),
            pl.BlockSpec((d, n), lambda l, s: (0, 0)),
            pl.BlockSpec((d, n), lambda l, s: (0, 0)),
            pl.BlockSpec((n_heads, n), lambda l, s: (0, 0)),
        ],
        out_shape=[
            jax.ShapeDtypeStruct((n_layers, n_hist, n, d), F32),
            jax.ShapeDtypeStruct((n, d), F32),
            jax.ShapeDtypeStruct((d, n), F32),
            jax.ShapeDtypeStruct((d, n), F32),
            jax.ShapeDtypeStruct((n_heads, n), F32),
        ],
        scratch_shapes=[pltpu.VMEM((n, d), F32)] * 4,
        compiler_params=_params("arbitrary", "arbitrary"),
        name="a_sample",
    )(x, st, cw4, g, w_in, cb, lg, lb, w_out, kvg, wk, wv, wft, fb_col, kg_col)


def _bias_sample_body(pt_ref, lfc_ref, lfst_ref, lower_ref, o_ref, *, n_pages):
    b = pl.program_id(0)
    pages = [lfc_ref[pt_ref[b, j]] for j in range(n_pages)]
    page = pages[0].shape[-1]
    n_heads = pages[0].shape[0]
    within = jnp.dot(jnp.concatenate(pages, axis=0), lower_ref[...], precision=lax.Precision.HIGHEST,
                     preferred_element_type=F32)
    lane = lax.broadcasted_iota(jnp.int32, lfst_ref.shape, 1)
    carry = jnp.sum(jnp.where(lane == b, lfst_ref[...], 0.0), axis=1, keepdims=True)
    for j in reversed(range(n_pages)):
        o_ref[0, :, j * page:(j + 1) * page] = within[j * n_heads:(j + 1) * n_heads] + carry
        carry = carry + jnp.sum(pages[j], axis=1, keepdims=True)


def _bias_sample(page_table, lfc, lfst):
    n, n_pages = page_table.shape
    n_pool, n_heads, page = lfc.shape
    lower = jnp.tril(jnp.ones((page, page), F32), k=-1)
    return pl.pallas_call(
        functools.partial(_bias_sample_body, n_pages=n_pages),
        grid_spec=pltpu.PrefetchScalarGridSpec(
            num_scalar_prefetch=1,
            grid=(n,),
            in_specs=[_const_spec((n_pool, n_heads, page)), _const_spec((n_heads, n)), _const_spec((page, page))],
            out_specs=pl.BlockSpec((1, n_heads, n_pages * page), lambda i, pt: (i, 0, 0)),
        ),
        out_shape=jax.ShapeDtypeStruct((n, n_heads, n_pages * page), F32),
        compiler_params=_params("arbitrary"),
        name="bias_sample",
    )(page_table, lfc, lfst, lower)


def _b_sample_body(pt_ref, x_ref, g_ref, win_ref, qg_ref, wout_ref, bias_ref, kst_ref, vst_ref, *rest,
                   n_heads, n_pages):
    k_refs = rest[:n_pages]
    v_refs = rest[n_pages:2 * n_pages]
    o_ref, q_sc, sz_sc, o_sc = rest[2 * n_pages:]
    n, d = x_ref.shape
    b = pl.program_id(0)

    @pl.when(b == 0)
    def _():
        xn = _rms_rows(x_ref[...], g_ref[...]).astype(BF16)
        u = jnp.dot(xn, win_ref[...], preferred_element_type=F32)
        scale = 1.0 / math.sqrt(HEAD_DIM)
        q_sc[...] = (_head_rms_t(u[:, :d].T, qg_ref[...], n_heads) * scale).T
        sz_sc[...] = _silu(u[:, d:])

    own = (lax.broadcasted_iota(jnp.int32, (n_heads, d), 1) // HEAD_DIM
           == lax.broadcasted_iota(jnp.int32, (n_heads, d), 0))
    qbd = jnp.where(own, jnp.broadcast_to(q_sc[pl.ds(b, 1), :], (n_heads, d)), 0.0).astype(BF16)

    s = jnp.concatenate(
        [jnp.dot(qbd, k_refs[j][0].astype(BF16), preferred_element_type=F32) for j in range(n_pages)],
        axis=1) + bias_ref[0]
    is_b = lax.broadcasted_iota(jnp.int32, (n_heads, n), 1) == b
    k_self = jnp.where(lax.broadcasted_iota(jnp.int32, (d, n), 1) == b, kst_ref[...], 0.0).astype(BF16)
    s_self = jnp.where(is_b, jnp.dot(qbd, k_self, preferred_element_type=F32), NEG_BIG)

    m = jnp.maximum(jnp.max(s, axis=1, keepdims=True), jnp.max(s_self, axis=1, keepdims=True))
    p = jnp.exp(s - m)
    p_self = jnp.exp(s_self - m)
    l = jnp.sum(p, axis=1, keepdims=True) + jnp.sum(p_self, axis=1, keepdims=True)
    pb = p.astype(BF16)
    page = s_self.shape[-1]
    nt = (((1,), (1,)), ((), ()))
    o = lax.dot_general(p_self.astype(BF16), vst_ref[...].astype(BF16), nt, preferred_element_type=F32)
    for j in range(n_pages):
        o = o + lax.dot_general(pb[:, j * page:(j + 1) * page], v_refs[j][0].astype(BF16), nt,
                                preferred_element_type=F32)
    o_sc[pl.ds(b, 1), :] = jnp.sum(jnp.where(own, o / l, 0.0), axis=0, keepdims=True)

    @pl.when(b == pl.num_programs(0) - 1)
    def _():
        mm = (o_sc[...] * sz_sc[...]).astype(BF16)
        o_ref[...] = x_ref[...] + jnp.dot(mm, wout_ref[...], preferred_element_type=F32)


def _b_sample(page_table, x, g, w_in, qg_col, w_out, bias, kst, vst, ckt, cvt):
    n, d = x.shape
    n_pages = page_table.shape[1]
    n_heads = bias.shape[1]
    page = ckt.shape[-1]
    assert page == n, "the new token's key rides through the kernel as one more page-sized block"
    page_spec = lambda j: pl.BlockSpec((1, d, page), lambda i, pt, j=j: (pt[i, j], 0, 0))
    return pl.pallas_call(
        functools.partial(_b_sample_body, n_heads=n_heads, n_pages=n_pages),
        grid_spec=pltpu.PrefetchScalarGridSpec(
            num_scalar_prefetch=1,
            grid=(n,),
            in_specs=[
                _const_spec((n, d)), _const_spec((1, d)), _const_spec((d, 2 * d)), _const_spec((HEAD_DIM, 1)),
                _const_spec((d, d)),
                pl.BlockSpec((1, n_heads, n_pages * page), lambda i, pt: (i, 0, 0)),
                _const_spec((d, n)), _const_spec((d, n)),
            ] + [page_spec(j) for j in range(n_pages)] * 2,
            out_specs=pl.BlockSpec((n, d), lambda i, pt: (0, 0)),
            scratch_shapes=[pltpu.VMEM((n, d), F32)] * 3,
        ),
        out_shape=jax.ShapeDtypeStruct((n, d), F32),
        compiler_params=_params("arbitrary"),
        name="b_sample",
    )(page_table, x, g, w_in, qg_col, w_out, bias, kst, vst, *([ckt] * n_pages), *([cvt] * n_pages))


def kernel(x_prompt, x_sample, state_conv, cache_k, cache_v, cache_logf, page_table, a_norm, a_w_in, a_conv_w,
           a_conv_b, a_ln_g, a_ln_b, a_w_out, kv_norm, kv_w, kv_fb, k_norm, b_norm, b_w_in, q_norm, b_w_out):
    bp, tp, d = x_prompt.shape
    bs = x_sample.shape[0]
    n_a = a_w_in.shape[0]
    n_b = b_w_in.shape[0]
    n_heads = kv_fb.shape[0]
    attn = n_heads * HEAD_DIM
    n_pool, page = cache_k.shape[0], cache_k.shape[1]

    a_w_in_b = a_w_in.astype(BF16)
    a_w_out_b = a_w_out.astype(BF16)
    b_w_in_b = b_w_in.astype(BF16)
    b_w_out_b = b_w_out.astype(BF16)
    wk = kv_w[:, :attn].astype(BF16)
    wv = kv_w[:, attn:2 * attn].astype(BF16)
    wft = kv_w[:, 2 * attn:].T.astype(BF16)
    fb_col = kv_fb.reshape(n_heads, 1)
    kg_col = k_norm.reshape(HEAD_DIM, 1)
    kvg = kv_norm.reshape(1, d)

    xp = x_prompt
    conv_p = []
    for i in range(n_a):
        xp, st = _a_prompt(xp, a_norm[i][None], a_w_in_b[i], a_conv_w[i], a_conv_b[i][None], a_ln_g[i][None],
                           a_ln_b[i][None], a_w_out_b[i])
        conv_p.append(st)
    kt, vt, lft, kb, vtb, ct = _kv_prompt(xp, kvg, wk, wv, wft, fb_col, kg_col)
    for j in range(n_b):
        xp = _b_prompt(xp, b_norm[j][None], b_w_in_b[j], q_norm[j].reshape(HEAD_DIM, 1), b_w_out_b[j],
                       kb, vtb, ct)

    st_t = jnp.transpose(state_conv, (0, 2, 1, 3))
    nst_t, xs, kst, vst, lfst = _a_sample(
        x_sample.reshape(bs, d), st_t, a_conv_w[:, :, None, :], a_norm[:, None], a_w_in_b, a_conv_b[:, None],
        a_ln_g[:, None], a_ln_b[:, None], a_w_out_b, kvg, wk, wv, wft, fb_col, kg_col)
    ckt = jnp.transpose(cache_k, (0, 2, 3, 1)).reshape(n_pool, attn, page)
    cvt = jnp.transpose(cache_v, (0, 2, 3, 1)).reshape(n_pool, attn, page)
    lfc = jnp.transpose(cache_logf, (0, 2, 1))
    bias = _bias_sample(page_table, lfc, lfst)
    for j in range(n_b):
        xs = _b_sample(page_table, xs, b_norm[j][None], b_w_in_b[j], q_norm[j].reshape(HEAD_DIM, 1),
                       b_w_out_b[j], bias, kst, vst, ckt, cvt)

    heads_last = lambda a: jnp.transpose(a.reshape(a.shape[0], n_heads, HEAD_DIM, a.shape[-1]), (0, 3, 1, 2))
    return (
        xp,
        xs.reshape(bs, 1, d),
        jnp.stack(conv_p, axis=0),
        jnp.transpose(nst_t, (0, 2, 1, 3)),
        heads_last(kt),
        heads_last(vt),
        jnp.transpose(lft, (0, 2, 1)),
        jnp.transpose(kst.reshape(n_heads, HEAD_DIM, bs), (2, 0, 1))[:, None],
        jnp.transpose(vst.reshape(n_heads, HEAD_DIM, bs), (2, 0, 1))[:, None],
        jnp.transpose(lfst, (1, 0))[:, None],
    )
```

```python
import functools
import math

import jax
import jax.numpy as jnp
from jax import lax
from jax.experimental import pallas as pl
from jax.experimental.pallas import tpu as pltpu

F32 = jnp.float32
BF16 = jnp.bfloat16
EPS = 1e-6
HEAD_DIM = 64
CONV_WIDTH = 31
CONV_HALO = 32
SUBLANES = 8
LANES = 128
NEG_BIG = -1e30
V7X_VMEM_LIMIT_BYTES = 56 * 2**20

A_TILE = 128
CONV_ROWS = 128
KV_TILE = 256
Q_TILE = 256
LOG2E = 1.4426950408889634
N_SPLIT = 3


def _rms_rows(x, g):
    return x * lax.rsqrt(jnp.mean(x * x, axis=-1, keepdims=True) + EPS) * g


def _silu(x):
    return x * jax.nn.sigmoid(x)


def _log_sigmoid(x):
    return -(jnp.maximum(-x, 0.0) + jnp.log1p(jnp.exp(-jnp.abs(x))))


def _layernorm_rows(x, g, b):
    xc = x - jnp.mean(x, axis=-1, keepdims=True)
    return xc * lax.rsqrt(jnp.mean(xc * xc, axis=-1, keepdims=True) + EPS) * g + b


def _head_rms_t(xt, g_col, n_heads):
    n = xt.shape[-1]
    x3 = xt.reshape(n_heads, HEAD_DIM, n)
    ms = jnp.mean(x3 * x3, axis=1, keepdims=True)
    return (x3 * lax.rsqrt(ms + EPS) * g_col[None]).reshape(n_heads * HEAD_DIM, n)


def _const_spec(shape):
    nd = len(shape)
    return pl.BlockSpec(shape, lambda *_: (0,) * nd, pipeline_mode=pl.Buffered(1))


def _params(*semantics):
    return pltpu.CompilerParams(dimension_semantics=semantics, vmem_limit_bytes=V7X_VMEM_LIMIT_BYTES)


def _conv_chunk(hist_ref, cw_ref, r0, rows, lsl):
    acc = None
    for phase in range(SUBLANES):
        offs = [o for o in range(2, CONV_HALO + 1) if o % SUBLANES == phase]
        n = rows if phase == 0 else rows + SUBLANES
        part = None
        for o in offs:
            term = cw_ref[o - 2:o - 1, lsl] * hist_ref[pl.ds(r0 + (o - phase), n), lsl]
            part = term if part is None else part + term
        part = part if phase == 0 else part[phase:phase + rows]
        acc = part if acc is None else acc + part
    return acc


def _a_tile_project(x, first, g_ref, win_ref, hist_ref, sz_ref):
    tt, d = x.shape

    @pl.when(first)
    def _():
        hist_ref[0:CONV_HALO, :] = jnp.zeros((CONV_HALO, d), F32)

    xn = _rms_rows(x, g_ref[...]).astype(BF16)
    u = jnp.dot(xn, win_ref[...], preferred_element_type=F32)
    hist_ref[CONV_HALO:CONV_HALO + tt, :] = u[:, :d] * jax.nn.sigmoid(u[:, d:2 * d])
    sz_ref[...] = _silu(u[:, 2 * d:])


def _a_tile_conv_piece(piece, hist_ref, cw_ref, yc_ref):
    n_lane_blocks = yc_ref.shape[1] // LANES
    c, l = divmod(piece, n_lane_blocks)
    lsl = slice(l * LANES, (l + 1) * LANES)
    yc_ref[c * CONV_ROWS:(c + 1) * CONV_ROWS, lsl] = _conv_chunk(hist_ref, cw_ref, c * CONV_ROWS, CONV_ROWS, lsl)


def _a_tile_finish(x, cb_ref, lg_ref, lb_ref, wout_ref, sz_ref, yc_ref):
    y = _silu(_layernorm_rows(yc_ref[...] + cb_ref[...], lg_ref[...], lb_ref[...]))
    m = (y * sz_ref[...]).astype(BF16)
    return x + jnp.dot(m, wout_ref[...], preferred_element_type=F32)


def _split_bf16(x):
    parts = []
    for _ in range(N_SPLIT - 1):
        hi = x.astype(BF16).astype(F32)
        parts.append(hi)
        x = x - hi
    return parts + [x]


def _kv_prompt_body(x_ref, g_ref, wk_ref, wv_ref, wft_ref, fb_ref, kg_ref, triu_ref,
                    kt_ref, vt_ref, lft_ref, kb_ref, vtb_ref, ct_ref, carry_ref, *, n_heads):
    t = pl.program_id(1)

    @pl.when(t == 0)
    def _():
        carry_ref[...] = jnp.zeros_like(carry_ref)

    xn = _rms_rows(x_ref[0], g_ref[...]).astype(BF16)
    tt = xn.shape[0]
    k = jnp.dot(xn, wk_ref[...], preferred_element_type=F32)
    v = jnp.dot(xn, wv_ref[...], preferred_element_type=F32)
    knt = _head_rms_t(k.T, kg_ref[...], n_heads)
    vt = v.T
    kt_ref[0] = knt
    vt_ref[0] = vt
    vtb_ref[0, 0] = vt.astype(BF16)

    uft = lax.dot_general(wft_ref[...], xn, (((1,), (1,)), ((), ())), preferred_element_type=F32)
    lft = _log_sigmoid(uft + fb_ref[...])
    lft_ref[0] = lft
    ct = jnp.dot(lft, triu_ref[...], precision=lax.Precision.HIGHEST,
                 preferred_element_type=F32) + carry_ref[...]
    ct_ref[0] = ct
    carry_ref[...] = ct[:, tt - 1:tt]

    terms = _split_bf16(ct * (-LOG2E))
    row = lax.broadcasted_iota(jnp.int32, (SUBLANES, tt), 0)
    pad = jnp.zeros((HEAD_DIM - SUBLANES, tt), F32)
    blocks = []
    for h in range(n_heads):
        aug = jnp.zeros((SUBLANES, tt), F32)
        for i, term in enumerate(terms):
            aug = jnp.where(row == i, term[h:h + 1, :], aug)
        blocks += [knt[h * HEAD_DIM:(h + 1) * HEAD_DIM], aug, pad]
    kb_ref[0] = jnp.concatenate(blocks, axis=0).T.astype(BF16)


def _kv_prompt(x, g, wk, wv, wft, fb_col, kg_col):
    b, t, d = x.shape
    n_heads = wft.shape[0]
    tt = KV_TILE
    triu = jnp.triu(jnp.ones((tt, tt), F32))
    feat_major = lambda rows: pl.BlockSpec((1, rows, tt), lambda i, j: (i, 0, j))
    return pl.pallas_call(
        functools.partial(_kv_prompt_body, n_heads=n_heads),
        grid=(b, t // tt),
        in_specs=[
            pl.BlockSpec((1, tt, d), lambda i, j: (i, j, 0)),
            _const_spec((1, d)), _const_spec((d, d)), _const_spec((d, d)), _const_spec((n_heads, d)),
            _const_spec((n_heads, 1)), _const_spec((HEAD_DIM, 1)), _const_spec((tt, tt)),
        ],
        out_specs=[
            feat_major(d), feat_major(d), feat_major(n_heads),
            pl.BlockSpec((1, tt, 2 * d), lambda i, j: (i, j, 0)),
            pl.BlockSpec((1, 1, d, tt), lambda i, j: (i, j, 0, 0)),
            feat_major(n_heads),
        ],
        out_shape=[
            jax.ShapeDtypeStruct((b, d, t), F32),
            jax.ShapeDtypeStruct((b, d, t), F32),
            jax.ShapeDtypeStruct((b, n_heads, t), F32),
            jax.ShapeDtypeStruct((b, t, 2 * d), BF16),
            jax.ShapeDtypeStruct((b, t // tt, d, tt), BF16),
            jax.ShapeDtypeStruct((b, n_heads, t), F32),
        ],
        scratch_shapes=[pltpu.VMEM((n_heads, 1), F32)],
        compiler_params=_params("arbitrary", "arbitrary"),
        name="kv_prompt",
    )(x, g, wk, wv, wft, fb_col, kg_col, triu)


def _b_prompt_body(x_ref, g_ref, win_ref, qg_ref, wout_ref, kb_ref, vtb_ref, ctq_ref,
                   o_ref, qm_ref, ot_ref, sz_ref, m_ref, l_ref, acc_ref, *, n_heads, tq, tk):
    d = x_ref.shape[-1]
    i = pl.program_id(1)
    x = x_ref[0]
    xn = _rms_rows(x, g_ref[...]).astype(BF16)
    u = jnp.dot(xn, win_ref[...], preferred_element_type=F32)
    sz_ref[...] = _silu(u[:, d:])
    scale = LOG2E / math.sqrt(HEAD_DIM)
    qnt = (_head_rms_t(u[:, :d].T, qg_ref[...], n_heads) * scale).astype(BF16)
    ones = (lax.broadcasted_iota(jnp.int32, (HEAD_DIM, tq), 0) < N_SPLIT).astype(BF16)
    for h in range(n_heads):
        qm_ref[h, 0:HEAD_DIM, :] = qnt[h * HEAD_DIM:(h + 1) * HEAD_DIM]
        qm_ref[h, HEAD_DIM:2 * HEAD_DIM, :] = ones
        m_ref[h] = jnp.full((1, tq), NEG_BIG, F32)
        l_ref[h] = jnp.zeros((1, tq), F32)
        acc_ref[h] = jnp.zeros((HEAD_DIM, tq), F32)

    causal = (lax.broadcasted_iota(jnp.int32, (tk, tq), 0) <= lax.broadcasted_iota(jnp.int32, (tk, tq), 1))

    def step(j, carry, diagonal):
        k0 = pl.multiple_of(j * tk, tk)
        sts = [jnp.dot(kb_ref[0, pl.ds(k0, tk), h * 2 * HEAD_DIM:(h + 1) * 2 * HEAD_DIM], qm_ref[h],
                       preferred_element_type=F32) for h in range(n_heads)]
        ps, alphas = [], []
        for h, st in enumerate(sts):
            cq = ctq_ref[0, h:h + 1, :] * LOG2E
            if diagonal:
                st = jnp.where(causal, st, NEG_BIG)
            m = m_ref[h]
            m_new = jnp.maximum(m, jnp.max(st, axis=0, keepdims=True) + cq)
            alpha = jnp.exp2(m - m_new)
            p = jnp.exp2(st - (m_new - cq))
            m_ref[h] = m_new
            l_ref[h] = alpha * l_ref[h] + jnp.sum(p, axis=0, keepdims=True)
            ps.append(p.astype(BF16))
            alphas.append(alpha)
        for h, (p, alpha) in enumerate(zip(ps, alphas)):
            vh = vtb_ref[0, j, h * HEAD_DIM:(h + 1) * HEAD_DIM, :]
            acc_ref[h] = alpha * acc_ref[h] + jnp.dot(vh, p, preferred_element_type=F32)
        return carry

    lax.fori_loop(0, i, functools.partial(step, diagonal=False), 0)
    step(i, 0, True)
    for h in range(n_heads):
        ot_ref[h * HEAD_DIM:(h + 1) * HEAD_DIM, :] = acc_ref[h] / l_ref[h]

    m = (ot_ref[...].T * sz_ref[...]).astype(BF16)
    o_ref[0] = x + jnp.dot(m, wout_ref[...], preferred_element_type=F32)


def _b_prompt(x, g, w_in, qg_col, w_out, kb, vtb, ct):
    b, t, d = x.shape
    n_heads = ct.shape[1]
    tq, tk = Q_TILE, KV_TILE
    assert tq == tk and vtb.shape == (b, t // tk, d, tk)
    return pl.pallas_call(
        functools.partial(_b_prompt_body, n_heads=n_heads, tq=tq, tk=tk),
        grid=(b, t // tq),
        in_specs=[
            pl.BlockSpec((1, tq, d), lambda i, j: (i, j, 0)),
            _const_spec((1, d)), _const_spec((d, 2 * d)), _const_spec((HEAD_DIM, 1)), _const_spec((d, d)),
            pl.BlockSpec((1, t, 2 * d), lambda i, j: (i, 0, 0)),
            pl.BlockSpec((1, t // tk, d, tk), lambda i, j: (i, 0, 0, 0)),
            pl.BlockSpec((1, n_heads, tq), lambda i, j: (i, 0, j)),
        ],
        out_specs=pl.BlockSpec((1, tq, d), lambda i, j: (i, j, 0)),
        out_shape=jax.ShapeDtypeStruct((b, t, d), F32),
        scratch_shapes=[
            pltpu.VMEM((n_heads, 2 * HEAD_DIM, tq), BF16),
            pltpu.VMEM((d, tq), F32),
            pltpu.VMEM((tq, d), F32),
            pltpu.VMEM((n_heads, 1, tq), F32),
            pltpu.VMEM((n_heads, 1, tq), F32),
            pltpu.VMEM((n_heads, HEAD_DIM, tq), F32),
        ],
        compiler_params=_params("arbitrary", "arbitrary"),
        name="b_prompt",
    )(x, g, w_in, qg_col, w_out, kb, vtb, ct)


def _a_sample_body(x_ref, st_ref, cw_ref, g_ref, win_ref, cb_ref, lg_ref, lb_ref, wout_ref,
                   kvg_ref, wk_ref, wv_ref, wft_ref, fb_ref, kg_ref,
                   nst_ref, xo_ref, kst_ref, vst_ref, lfst_ref,
                   x_sc, glu_sc, sz_sc, acc_sc, *, n_heads):
    d = x_ref.shape[-1]
    layer = pl.program_id(0)
    s = pl.program_id(1)
    n_hist = CONV_WIDTH - 1

    @pl.when((layer == 0) & (s == 0))
    def _():
        x_sc[...] = x_ref[...]

    @pl.when(s == 0)
    def _():
        xn = _rms_rows(x_sc[...], g_ref[0]).astype(BF16)
        u = jnp.dot(xn, win_ref[0], preferred_element_type=F32)
        glu_sc[...] = u[:, :d] * jax.nn.sigmoid(u[:, d:2 * d])
        sz_sc[...] = _silu(u[:, 2 * d:])
        acc_sc[...] = jnp.zeros_like(acc_sc)

    @pl.when(s < n_hist)
    def _():
        acc_sc[...] += cw_ref[0, 0] * st_ref[0, 0]

    @pl.when((s >= 1) & (s < n_hist))
    def _():
        nst_ref[0, 0] = st_ref[0, 0]

    @pl.when(s == n_hist)
    def _():
        glu = glu_sc[...]
        nst_ref[0, 0] = glu
        yc = acc_sc[...] + cw_ref[0, 0] * glu + cb_ref[0]
        y = _silu(_layernorm_rows(yc, lg_ref[0], lb_ref[0]))
        m = (y * sz_sc[...]).astype(BF16)
        x_sc[...] = x_sc[...] + jnp.dot(m, wout_ref[0], preferred_element_type=F32)

    @pl.when((s == n_hist) & (layer == pl.num_programs(0) - 1))
    def _():
        x = x_sc[...]
        xo_ref[...] = x
        xn = _rms_rows(x, kvg_ref[...]).astype(BF16)
        k = jnp.dot(xn, wk_ref[...], preferred_element_type=F32)
        v = jnp.dot(xn, wv_ref[...], preferred_element_type=F32)
        kst_ref[...] = _head_rms_t(k.T, kg_ref[...], n_heads)
        vst_ref[...] = v.T
        uft = lax.dot_general(wft_ref[...], xn, (((1,), (1,)), ((), ())), preferred_element_type=F32)
        lfst_ref[...] = _log_sigmoid(uft + fb_ref[...])


def _a_sample(x, st, cw4, g, w_in, cb, lg, lb, w_out, kvg, wk, wv, wft, fb_col, kg_col):
    n, d = x.shape
    n_layers, n_hist = st.shape[0], st.shape[1]
    n_heads = wft.shape[0]
    per_layer = lambda *shape: pl.BlockSpec((1,) + shape, lambda l, s: (l,) + (0,) * len(shape))
    return pl.pallas_call(
        functools.partial(_a_sample_body, n_heads=n_heads),
        grid=(n_layers, n_hist + 1),
        in_specs=[
            _const_spec((n, d)),
            pl.BlockSpec((1, 1, n, d), lambda l, s: (l, jnp.minimum(s, n_hist - 1), 0, 0)),
            pl.BlockSpec((1, 1, 1, d), lambda l, s: (l, s, 0, 0)),
            per_layer(1, d), per_layer(d, 3 * d), per_layer(1, d), per_layer(1, d), per_layer(1, d),
            per_layer(d, d),
            _const_spec((1, d)), _const_spec((d, d)), _const_spec((d, d)), _const_spec((n_heads, d)),
            _const_spec((n_heads, 1)), _const_spec((HEAD_DIM, 1)),
        ],
        out_specs=[
            pl.BlockSpec((1, 1, n, d), lambda l, s: (l, jnp.maximum(s - 1, 0), 0, 0)),
            pl.BlockSpec((n, d), lambda l, s: (0, 0)),
            pl.BlockSpec((d, n), lambda l, s: (0, 0)),
            pl.BlockSpec((d, n), lambda l, s: (0, 0)),
            pl.BlockSpec((n_heads, n), lambda l, s: (0, 0)),
        ],
        out_shape=[
            jax.ShapeDtypeStruct((n_layers, n_hist, n, d), F32),
            jax.ShapeDtypeStruct((n, d), F32),
            jax.ShapeDtypeStruct((d, n), F32),
            jax.ShapeDtypeStruct((d, n), F32),
            jax.ShapeDtypeStruct((n_heads, n), F32),
        ],
        scratch_shapes=[pltpu.VMEM((n, d), F32)] * 4,
        compiler_params=_params("arbitrary", "arbitrary"),
        name="a_sample",
    )(x, st, cw4, g, w_in, cb, lg, lb, w_out, kvg, wk, wv, wft, fb_col, kg_col)


def _bias_sample_body(pt_ref, lfc_ref, lfst_ref, lower_ref, o_ref, *, n_pages):
    b = pl.program_id(0)
    pages = [lfc_ref[pt_ref[b, j]] for j in range(n_pages)]
    page = pages[0].shape[-1]
    n_heads = pages[0].shape[0]
    within = jnp.dot(jnp.concatenate(pages, axis=0), lower_ref[...], precision=lax.Precision.HIGHEST,
                     preferred_element_type=F32)
    lane = lax.broadcasted_iota(jnp.int32, lfst_ref.shape, 1)
    carry = jnp.sum(jnp.where(lane == b, lfst_ref[...], 0.0), axis=1, keepdims=True)
    for j in reversed(range(n_pages)):
        o_ref[0, :, j * page:(j + 1) * page] = within[j * n_heads:(j + 1) * n_heads] + carry
        carry = carry + jnp.sum(pages[j], axis=1, keepdims=True)


def _bias_sample(page_table, lfc, lfst):
    n, n_pages = page_table.shape
    n_pool, n_heads, page = lfc.shape
    lower = jnp.tril(jnp.ones((page, page), F32), k=-1)
    return pl.pallas_call(
        functools.partial(_bias_sample_body, n_pages=n_pages),
        grid_spec=pltpu.PrefetchScalarGridSpec(
            num_scalar_prefetch=1,
            grid=(n,),
            in_specs=[_const_spec((n_pool, n_heads, page)), _const_spec((n_heads, n)), _const_spec((page, page))],
            out_specs=pl.BlockSpec((1, n_heads, n_pages * page), lambda i, pt: (i, 0, 0)),
        ),
        out_shape=jax.ShapeDtypeStruct((n, n_heads, n_pages * page), F32),
        compiler_params=_params("arbitrary"),
        name="bias_sample",
    )(page_table, lfc, lfst, lower)


def _sample_scores(b, q_row, bias_ref, kst_ref, k_refs, n_heads):
    d = q_row.shape[-1]
    n = kst_ref.shape[-1]
    own = (lax.broadcasted_iota(jnp.int32, (n_heads, d), 1) // HEAD_DIM
           == lax.broadcasted_iota(jnp.int32, (n_heads, d), 0))
    qbd = jnp.where(own, jnp.broadcast_to(q_row, (n_heads, d)), 0.0).astype(BF16)
    s = jnp.concatenate(
        [jnp.dot(qbd, k_ref[0].astype(BF16), preferred_element_type=F32) for k_ref in k_refs],
        axis=1) + bias_ref[0]
    is_b = lax.broadcasted_iota(jnp.int32, (n_heads, n), 1) == b
    k_self = jnp.where(lax.broadcasted_iota(jnp.int32, (d, n), 1) == b, kst_ref[...], 0.0).astype(BF16)
    s_self = jnp.where(is_b, jnp.dot(qbd, k_self, preferred_element_type=F32), NEG_BIG)
    m = jnp.maximum(jnp.max(s, axis=1, keepdims=True), jnp.max(s_self, axis=1, keepdims=True))
    p = jnp.exp(s - m)
    p_self = jnp.exp(s_self - m)
    l = jnp.sum(p, axis=1, keepdims=True) + jnp.sum(p_self, axis=1, keepdims=True)
    return p.astype(BF16), p_self.astype(BF16), l, own


def _weights_times_values(pw, vt):
    return lax.dot_general(pw, vt.astype(BF16), (((1,), (1,)), ((), ())), preferred_element_type=F32)


def _ab_body(pt_ref, xs_ref, gb_ref, winb_ref, qg_ref, woutb_ref, bias_ref, kst_ref, vst_ref,
             xp_ref, ga_ref, wina_ref, cw_ref, cb_ref, lg_ref, lb_ref, wouta_ref, *rest,
             n_heads, n_pages, tiles_per_seq):
    k_refs = rest[:n_pages]
    v_refs = rest[n_pages:2 * n_pages]
    ys_ref, yp_ref, st_ref, q_sc, szs_sc, o_sc, hist_ref, sza_ref, yc_ref = rest[2 * n_pages:]
    n, d = xs_ref.shape
    tt = xp_ref.shape[1]
    i = pl.program_id(0)
    t = i % tiles_per_seq

    @pl.when(i == 0)
    def _():
        xn = _rms_rows(xs_ref[...], gb_ref[...]).astype(BF16)
        u = jnp.dot(xn, winb_ref[...], preferred_element_type=F32)
        scale = 1.0 / math.sqrt(HEAD_DIM)
        q_sc[...] = (_head_rms_t(u[:, :d].T, qg_ref[...], n_heads) * scale).T
        szs_sc[...] = _silu(u[:, d:])

    pb, p_self, l, own = _sample_scores(i, q_sc[pl.ds(i, 1), :], bias_ref, kst_ref, k_refs, n_heads)
    page = p_self.shape[-1]
    parts = [_weights_times_values(p_self, vst_ref[...])]
    for j, v_ref in enumerate(v_refs):
        parts.append(_weights_times_values(pb[:, j * page:(j + 1) * page], v_ref[0]))
    while len(parts) > 1:
        parts = [a + b for a, b in zip(parts[0::2], parts[1::2])] + ([parts[-1]] if len(parts) % 2 else [])
    o_sc[pl.ds(i, 1), :] = jnp.sum(jnp.where(own, parts[0] / l, 0.0), axis=0, keepdims=True)

    x = xp_ref[0]
    _a_tile_project(x, t == 0, ga_ref, wina_ref, hist_ref, sza_ref)
    for piece in range((tt // CONV_ROWS) * (d // LANES)):
        _a_tile_conv_piece(piece, hist_ref, cw_ref, yc_ref)
    yp_ref[0] = _a_tile_finish(x, cb_ref, lg_ref, lb_ref, wouta_ref, sza_ref, yc_ref)

    @pl.when(t == tiles_per_seq - 1)
    def _():
        st_ref[0] = hist_ref[tt + CONV_HALO - (CONV_WIDTH - 1):tt + CONV_HALO, :]

    hist_ref[0:CONV_HALO, :] = hist_ref[tt:tt + CONV_HALO, :]

    @pl.when(i == pl.num_programs(0) - 1)
    def _():
        mm = (o_sc[...] * szs_sc[...]).astype(BF16)
        ys_ref[...] = xs_ref[...] + jnp.dot(mm, woutb_ref[...], preferred_element_type=F32)


def _ab_layer(page_table, xs, gb, w_in_b, qg_col, w_out_b, bias, kst, vst, ckt, cvt,
              xp, ga, w_in_a, cw, cb, lg, lb, w_out_a):
    n, d = xs.shape
    bp, tp, _ = xp.shape
    n_pages = page_table.shape[1]
    n_heads = bias.shape[1]
    page = ckt.shape[-1]
    tt = A_TILE
    tiles_per_seq = tp // tt
    assert page == n, "the new token's key rides through the kernel as one more page-sized block"
    assert bp * tiles_per_seq == n, "one prompt tile per sample grid step"
    page_spec = lambda j: pl.BlockSpec((1, d, page), lambda i, pt, j=j: (pt[i, j], 0, 0))
    tile_spec = pl.BlockSpec((1, tt, d), lambda i, pt: (i // tiles_per_seq, i % tiles_per_seq, 0))
    row = lambda: _const_spec((1, d))
    return pl.pallas_call(
        functools.partial(_ab_body, n_heads=n_heads, n_pages=n_pages, tiles_per_seq=tiles_per_seq),
        grid_spec=pltpu.PrefetchScalarGridSpec(
            num_scalar_prefetch=1,
            grid=(n,),
            in_specs=[
                _const_spec((n, d)), row(), _const_spec((d, 2 * d)), _const_spec((HEAD_DIM, 1)),
                _const_spec((d, d)),
                pl.BlockSpec((1, n_heads, n_pages * page), lambda i, pt: (i, 0, 0)),
                _const_spec((d, n)), _const_spec((d, n)),
                tile_spec, row(), _const_spec((d, 3 * d)), _const_spec((CONV_WIDTH, d)), row(), row(), row(),
                _const_spec((d, d)),
            ] + [page_spec(j) for j in range(n_pages)] * 2,
            out_specs=[
                pl.BlockSpec((n, d), lambda i, pt: (0, 0)),
                tile_spec,
                pl.BlockSpec((1, CONV_WIDTH - 1, d), lambda i, pt: (i // tiles_per_seq, 0, 0)),
            ],
            scratch_shapes=[pltpu.VMEM((n, d), F32)] * 3 + [
                pltpu.VMEM((tt + CONV_HALO, d), F32), pltpu.VMEM((tt, d), F32), pltpu.VMEM((tt, d), F32)],
        ),
        out_shape=[
            jax.ShapeDtypeStruct((n, d), F32),
            jax.ShapeDtypeStruct((bp, tp, d), F32),
            jax.ShapeDtypeStruct((bp, CONV_WIDTH - 1, d), F32),
        ],
        compiler_params=_params("arbitrary"),
        name="ab_layer",
    )(page_table, xs, gb, w_in_b, qg_col, w_out_b, bias, kst, vst,
      xp, ga, w_in_a, cw, cb, lg, lb, w_out_a, *([ckt] * n_pages), *([cvt] * n_pages))


def kernel(x_prompt, x_sample, state_conv, cache_k, cache_v, cache_logf, page_table, a_norm, a_w_in, a_conv_w,
           a_conv_b, a_ln_g, a_ln_b, a_w_out, kv_norm, kv_w, kv_fb, k_norm, b_norm, b_w_in, q_norm, b_w_out):
    bp, tp, d = x_prompt.shape
    bs = x_sample.shape[0]
    n_a = a_w_in.shape[0]
    n_b = b_w_in.shape[0]
    n_heads = kv_fb.shape[0]
    attn = n_heads * HEAD_DIM
    n_pool, page = cache_k.shape[0], cache_k.shape[1]

    a_w_in_b = a_w_in.astype(BF16)
    a_w_out_b = a_w_out.astype(BF16)
    b_w_in_b = b_w_in.astype(BF16)
    b_w_out_b = b_w_out.astype(BF16)
    wk = kv_w[:, :attn].astype(BF16)
    wv = kv_w[:, attn:2 * attn].astype(BF16)
    wft = kv_w[:, 2 * attn:].T.astype(BF16)
    fb_col = kv_fb.reshape(n_heads, 1)
    kg_col = k_norm.reshape(HEAD_DIM, 1)
    kvg = kv_norm.reshape(1, d)

    st_t = jnp.transpose(state_conv, (0, 2, 1, 3))
    nst_t, xs, kst, vst, lfst = _a_sample(
        x_sample.reshape(bs, d), st_t, a_conv_w[:, :, None, :], a_norm[:, None], a_w_in_b, a_conv_b[:, None],
        a_ln_g[:, None], a_ln_b[:, None], a_w_out_b, kvg, wk, wv, wft, fb_col, kg_col)
    ckt = jnp.transpose(cache_k, (0, 2, 3, 1)).reshape(n_pool, attn, page)
    cvt = jnp.transpose(cache_v, (0, 2, 3, 1)).reshape(n_pool, attn, page)
    lfc = jnp.transpose(cache_logf, (0, 2, 1))
    bias = _bias_sample(page_table, lfc, lfst)

    assert n_a == n_b
    xp = x_prompt
    conv_p = []
    for j in range(n_b):
        xs, xp, st = _ab_layer(
            page_table, xs, b_norm[j][None], b_w_in_b[j], q_norm[j].reshape(HEAD_DIM, 1), b_w_out_b[j],
            bias, kst, vst, ckt, cvt,
            xp, a_norm[j][None], a_w_in_b[j], a_conv_w[j], a_conv_b[j][None], a_ln_g[j][None], a_ln_b[j][None],
            a_w_out_b[j])
        conv_p.append(st)

    kt, vt, lft, kb, vtb, ct = _kv_prompt(xp, kvg, wk, wv, wft, fb_col, kg_col)
    for j in range(n_b):
        xp = _b_prompt(xp, b_norm[j][None], b_w_in_b[j], q_norm[j].reshape(HEAD_DIM, 1), b_w_out_b[j],
                       kb, vtb, ct)

    heads_last = lambda a: jnp.transpose(a.reshape(a.shape[0], n_heads, HEAD_DIM, a.shape[-1]), (0, 3, 1, 2))
    return (
        xp,
        xs.reshape(bs, 1, d),
        jnp.stack(conv_p, axis=0),
        jnp.transpose(nst_t, (0, 2, 1, 3)),
        heads_last(kt),
        heads_last(vt),
        jnp.transpose(lft, (0, 2, 1)),
        jnp.transpose(kst.reshape(n_heads, HEAD_DIM, bs), (2, 0, 1))[:, None],
        jnp.transpose(vst.reshape(n_heads, HEAD_DIM, bs), (2, 0, 1))[:, None],
        jnp.transpose(lfst, (1, 0))[:, None],
    )
```

```python
import functools
import math

import jax
import jax.numpy as jnp
from jax import lax
from jax.experimental import pallas as pl
from jax.experimental.pallas import tpu as pltpu

F32 = jnp.float32
BF16 = jnp.bfloat16
EPS = 1e-6
HEAD_DIM = 64
CONV_WIDTH = 31
CONV_HALO = 32
SUBLANES = 8
LANES = 128
NEG_BIG = -1e30
V7X_VMEM_LIMIT_BYTES = 56 * 2**20

A_TILE = 128
CONV_ROWS = 128
KV_TILE = 256
Q_TILE = 256
LOG2E = 1.4426950408889634
N_SPLIT = 3
BIAS_GROUP = 8


def _rms_rows(x, g):
    return x * lax.rsqrt(jnp.mean(x * x, axis=-1, keepdims=True) + EPS) * g


def _silu(x):
    return x * jax.nn.sigmoid(x)


def _log_sigmoid(x):
    return -(jnp.maximum(-x, 0.0) + jnp.log1p(jnp.exp(-jnp.abs(x))))


def _layernorm_rows(x, g, b):
    xc = x - jnp.mean(x, axis=-1, keepdims=True)
    return xc * lax.rsqrt(jnp.mean(xc * xc, axis=-1, keepdims=True) + EPS) * g + b


def _head_rms_t(xt, g_col, n_heads):
    n = xt.shape[-1]
    x3 = xt.reshape(n_heads, HEAD_DIM, n)
    ms = jnp.mean(x3 * x3, axis=1, keepdims=True)
    return (x3 * lax.rsqrt(ms + EPS) * g_col[None]).reshape(n_heads * HEAD_DIM, n)


def _const_spec(shape):
    nd = len(shape)
    return pl.BlockSpec(shape, lambda *_: (0,) * nd, pipeline_mode=pl.Buffered(1))


def _params(*semantics):
    return pltpu.CompilerParams(dimension_semantics=semantics, vmem_limit_bytes=V7X_VMEM_LIMIT_BYTES)


def _conv_chunk(hist_ref, cw_ref, r0, rows, lsl):
    acc = None
    for phase in range(SUBLANES):
        offs = [o for o in range(2, CONV_HALO + 1) if o % SUBLANES == phase]
        n = rows if phase == 0 else rows + SUBLANES
        part = None
        for o in offs:
            term = cw_ref[o - 2:o - 1, lsl] * hist_ref[pl.ds(r0 + (o - phase), n), lsl]
            part = term if part is None else part + term
        part = part if phase == 0 else part[phase:phase + rows]
        acc = part if acc is None else acc + part
    return acc


def _a_tile_project(x, first, g_ref, win_ref, hist_ref, sz_ref):
    tt, d = x.shape

    @pl.when(first)
    def _():
        hist_ref[0:CONV_HALO, :] = jnp.zeros((CONV_HALO, d), F32)

    xn = _rms_rows(x, g_ref[...]).astype(BF16)
    u = jnp.dot(xn, win_ref[...], preferred_element_type=F32)
    hist_ref[CONV_HALO:CONV_HALO + tt, :] = u[:, :d] * jax.nn.sigmoid(u[:, d:2 * d])
    sz_ref[...] = _silu(u[:, 2 * d:])


def _a_tile_conv_piece(piece, hist_ref, cw_ref, yc_ref):
    n_lane_blocks = yc_ref.shape[1] // LANES
    c, l = divmod(piece, n_lane_blocks)
    lsl = slice(l * LANES, (l + 1) * LANES)
    yc_ref[c * CONV_ROWS:(c + 1) * CONV_ROWS, lsl] = _conv_chunk(hist_ref, cw_ref, c * CONV_ROWS, CONV_ROWS, lsl)


def _a_tile_finish(x, cb_ref, lg_ref, lb_ref, wout_ref, sz_ref, yc_ref):
    y = _silu(_layernorm_rows(yc_ref[...] + cb_ref[...], lg_ref[...], lb_ref[...]))
    m = (y * sz_ref[...]).astype(BF16)
    return x + jnp.dot(m, wout_ref[...], preferred_element_type=F32)


def _split_bf16(x):
    parts = []
    for _ in range(N_SPLIT - 1):
        hi = x.astype(BF16).astype(F32)
        parts.append(hi)
        x = x - hi
    return parts + [x]


def _kv_prompt_body(x_ref, g_ref, wk_ref, wv_ref, wft_ref, fb_ref, kg_ref, triu_ref,
                    kt_ref, vt_ref, lft_ref, kb_ref, vtb_ref, ct_ref, carry_ref, *, n_heads):
    t = pl.program_id(1)

    @pl.when(t == 0)
    def _():
        carry_ref[...] = jnp.zeros_like(carry_ref)

    xn = _rms_rows(x_ref[0], g_ref[...]).astype(BF16)
    tt = xn.shape[0]
    k = jnp.dot(xn, wk_ref[...], preferred_element_type=F32)
    v = jnp.dot(xn, wv_ref[...], preferred_element_type=F32)
    knt = _head_rms_t(k.T, kg_ref[...], n_heads)
    vt = v.T
    kt_ref[0] = knt
    vt_ref[0] = vt
    vtb_ref[0, 0] = vt.astype(BF16)

    uft = lax.dot_general(wft_ref[...], xn, (((1,), (1,)), ((), ())), preferred_element_type=F32)
    lft = _log_sigmoid(uft + fb_ref[...])
    lft_ref[0] = lft
    ct = jnp.dot(lft, triu_ref[...], precision=lax.Precision.HIGHEST,
                 preferred_element_type=F32) + carry_ref[...]
    ct_ref[0] = ct
    carry_ref[...] = ct[:, tt - 1:tt]

    terms = _split_bf16(ct * (-LOG2E))
    row = lax.broadcasted_iota(jnp.int32, (SUBLANES, tt), 0)
    pad = jnp.zeros((HEAD_DIM - SUBLANES, tt), F32)
    blocks = []
    for h in range(n_heads):
        aug = jnp.zeros((SUBLANES, tt), F32)
        for i, term in enumerate(terms):
            aug = jnp.where(row == i, term[h:h + 1, :], aug)
        blocks += [knt[h * HEAD_DIM:(h + 1) * HEAD_DIM], aug, pad]
    kb_ref[0] = jnp.concatenate(blocks, axis=0).T.astype(BF16)


def _kv_prompt(x, g, wk, wv, wft, fb_col, kg_col):
    b, t, d = x.shape
    n_heads = wft.shape[0]
    tt = KV_TILE
    triu = jnp.triu(jnp.ones((tt, tt), F32))
    feat_major = lambda rows: pl.BlockSpec((1, rows, tt), lambda i, j: (i, 0, j))
    return pl.pallas_call(
        functools.partial(_kv_prompt_body, n_heads=n_heads),
        grid=(b, t // tt),
        in_specs=[
            pl.BlockSpec((1, tt, d), lambda i, j: (i, j, 0)),
            _const_spec((1, d)), _const_spec((d, d)), _const_spec((d, d)), _const_spec((n_heads, d)),
            _const_spec((n_heads, 1)), _const_spec((HEAD_DIM, 1)), _const_spec((tt, tt)),
        ],
        out_specs=[
            feat_major(d), feat_major(d), feat_major(n_heads),
            pl.BlockSpec((1, tt, 2 * d), lambda i, j: (i, j, 0)),
            pl.BlockSpec((1, 1, d, tt), lambda i, j: (i, j, 0, 0)),
            feat_major(n_heads),
        ],
        out_shape=[
            jax.ShapeDtypeStruct((b, d, t), F32),
            jax.ShapeDtypeStruct((b, d, t), F32),
            jax.ShapeDtypeStruct((b, n_heads, t), F32),
            jax.ShapeDtypeStruct((b, t, 2 * d), BF16),
            jax.ShapeDtypeStruct((b, t // tt, d, tt), BF16),
            jax.ShapeDtypeStruct((b, n_heads, t), F32),
        ],
        scratch_shapes=[pltpu.VMEM((n_heads, 1), F32)],
        compiler_params=_params("arbitrary", "arbitrary"),
        name="kv_prompt",
    )(x, g, wk, wv, wft, fb_col, kg_col, triu)


def _b_prompt_body(x_ref, g_ref, win_ref, qg_ref, wout_ref, kb_ref, vtb_ref, ctq_ref,
                   o_ref, qm_ref, ot_ref, sz_ref, m_ref, l_ref, acc_ref, *, n_heads, tq, tk):
    d = x_ref.shape[-1]
    i = pl.program_id(1)
    x = x_ref[0]
    xn = _rms_rows(x, g_ref[...]).astype(BF16)
    u = jnp.dot(xn, win_ref[...], preferred_element_type=F32)
    sz_ref[...] = _silu(u[:, d:])
    scale = LOG2E / math.sqrt(HEAD_DIM)
    qnt = (_head_rms_t(u[:, :d].T, qg_ref[...], n_heads) * scale).astype(BF16)
    ones = (lax.broadcasted_iota(jnp.int32, (HEAD_DIM, tq), 0) < N_SPLIT).astype(BF16)
    for h in range(n_heads):
        qm_ref[h, 0:HEAD_DIM, :] = qnt[h * HEAD_DIM:(h + 1) * HEAD_DIM]
        qm_ref[h, HEAD_DIM:2 * HEAD_DIM, :] = ones
        m_ref[h] = jnp.full((1, tq), NEG_BIG, F32)
        l_ref[h] = jnp.zeros((1, tq), F32)
        acc_ref[h] = jnp.zeros((HEAD_DIM, tq), F32)

    causal = (lax.broadcasted_iota(jnp.int32, (tk, tq), 0) <= lax.broadcasted_iota(jnp.int32, (tk, tq), 1))

    def step(j, carry, diagonal):
        k0 = pl.multiple_of(j * tk, tk)
        sts = [jnp.dot(kb_ref[0, pl.ds(k0, tk), h * 2 * HEAD_DIM:(h + 1) * 2 * HEAD_DIM], qm_ref[h],
                       preferred_element_type=F32) for h in range(n_heads)]
        ps, alphas = [], []
        for h, st in enumerate(sts):
            cq = ctq_ref[0, h:h + 1, :] * LOG2E
            if diagonal:
                st = jnp.where(causal, st, NEG_BIG)
            m = m_ref[h]
            m_new = jnp.maximum(m, jnp.max(st, axis=0, keepdims=True) + cq)
            alpha = jnp.exp2(m - m_new)
            p = jnp.exp2(st - (m_new - cq))
            m_ref[h] = m_new
            l_ref[h] = alpha * l_ref[h] + jnp.sum(p, axis=0, keepdims=True)
            ps.append(p.astype(BF16))
            alphas.append(alpha)
        for h, (p, alpha) in enumerate(zip(ps, alphas)):
            vh = vtb_ref[0, j, h * HEAD_DIM:(h + 1) * HEAD_DIM, :]
            acc_ref[h] = alpha * acc_ref[h] + jnp.dot(vh, p, preferred_element_type=F32)
        return carry

    lax.fori_loop(0, i, functools.partial(step, diagonal=False), 0)
    step(i, 0, True)
    for h in range(n_heads):
        ot_ref[h * HEAD_DIM:(h + 1) * HEAD_DIM, :] = acc_ref[h] / l_ref[h]

    m = (ot_ref[...].T * sz_ref[...]).astype(BF16)
    o_ref[0] = x + jnp.dot(m, wout_ref[...], preferred_element_type=F32)


def _b_prompt(x, g, w_in, qg_col, w_out, kb, vtb, ct):
    b, t, d = x.shape
    n_heads = ct.shape[1]
    tq, tk = Q_TILE, KV_TILE
    assert tq == tk and vtb.shape == (b, t // tk, d, tk)
    return pl.pallas_call(
        functools.partial(_b_prompt_body, n_heads=n_heads, tq=tq, tk=tk),
        grid=(b, t // tq),
        in_specs=[
            pl.BlockSpec((1, tq, d), lambda i, j: (i, j, 0)),
            _const_spec((1, d)), _const_spec((d, 2 * d)), _const_spec((HEAD_DIM, 1)), _const_spec((d, d)),
            pl.BlockSpec((1, t, 2 * d), lambda i, j: (i, 0, 0)),
            pl.BlockSpec((1, t // tk, d, tk), lambda i, j: (i, 0, 0, 0)),
            pl.BlockSpec((1, n_heads, tq), lambda i, j: (i, 0, j)),
        ],
        out_specs=pl.BlockSpec((1, tq, d), lambda i, j: (i, j, 0)),
        out_shape=jax.ShapeDtypeStruct((b, t, d), F32),
        scratch_shapes=[
            pltpu.VMEM((n_heads, 2 * HEAD_DIM, tq), BF16),
            pltpu.VMEM((d, tq), F32),
            pltpu.VMEM((tq, d), F32),
            pltpu.VMEM((n_heads, 1, tq), F32),
            pltpu.VMEM((n_heads, 1, tq), F32),
            pltpu.VMEM((n_heads, HEAD_DIM, tq), F32),
        ],
        compiler_params=_params("arbitrary", "arbitrary"),
        name="b_prompt",
    )(x, g, w_in, qg_col, w_out, kb, vtb, ct)


def _a_sample_body(x_ref, st_ref, cw_ref, g_ref, win_ref, cb_ref, lg_ref, lb_ref, wout_ref,
                   kvg_ref, wk_ref, wv_ref, wft_ref, fb_ref, kg_ref,
                   nst_ref, xo_ref, kst_ref, vst_ref, lfst_ref,
                   x_sc, glu_sc, sz_sc, acc_sc, *, n_heads):
    d = x_ref.shape[-1]
    layer = pl.program_id(0)
    s = pl.program_id(1)
    n_hist = CONV_WIDTH - 1

    @pl.when((layer == 0) & (s == 0))
    def _():
        x_sc[...] = x_ref[...]

    @pl.when(s == 0)
    def _():
        xn = _rms_rows(x_sc[...], g_ref[0]).astype(BF16)
        u = jnp.dot(xn, win_ref[0], preferred_element_type=F32)
        glu_sc[...] = u[:, :d] * jax.nn.sigmoid(u[:, d:2 * d])
        sz_sc[...] = _silu(u[:, 2 * d:])
        acc_sc[...] = jnp.zeros_like(acc_sc)

    @pl.when(s < n_hist)
    def _():
        acc_sc[...] += cw_ref[0, 0] * st_ref[0, 0]

    @pl.when((s >= 1) & (s < n_hist))
    def _():
        nst_ref[0, 0] = st_ref[0, 0]

    @pl.when(s == n_hist)
    def _():
        glu = glu_sc[...]
        nst_ref[0, 0] = glu
        yc = acc_sc[...] + cw_ref[0, 0] * glu + cb_ref[0]
        y = _silu(_layernorm_rows(yc, lg_ref[0], lb_ref[0]))
        m = (y * sz_sc[...]).astype(BF16)
        x_sc[...] = x_sc[...] + jnp.dot(m, wout_ref[0], preferred_element_type=F32)

    @pl.when((s == n_hist) & (layer == pl.num_programs(0) - 1))
    def _():
        x = x_sc[...]
        xo_ref[...] = x
        xn = _rms_rows(x, kvg_ref[...]).astype(BF16)
        k = jnp.dot(xn, wk_ref[...], preferred_element_type=F32)
        v = jnp.dot(xn, wv_ref[...], preferred_element_type=F32)
        kst_ref[...] = _head_rms_t(k.T, kg_ref[...], n_heads)
        vst_ref[...] = v.T
        uft = lax.dot_general(wft_ref[...], xn, (((1,), (1,)), ((), ())), preferred_element_type=F32)
        lfst_ref[...] = _log_sigmoid(uft + fb_ref[...])


def _a_sample(x, st, cw4, g, w_in, cb, lg, lb, w_out, kvg, wk, wv, wft, fb_col, kg_col):
    n, d = x.shape
    n_layers, n_hist = st.shape[0], st.shape[1]
    n_heads = wft.shape[0]
    per_layer = lambda *shape: pl.BlockSpec((1,) + shape, lambda l, s: (l,) + (0,) * len(shape))
    return pl.pallas_call(
        functools.partial(_a_sample_body, n_heads=n_heads),
        grid=(n_layers, n_hist + 1),
        in_specs=[
            _const_spec((n, d)),
            pl.BlockSpec((1, 1, n, d), lambda l, s: (l, jnp.minimum(s, n_hist - 1), 0, 0)),
            pl.BlockSpec((1, 1, 1, d), lambda l, s: (l, s, 0, 0)),
            per_layer(1, d), per_layer(d, 3 * d), per_layer(1, d), per_layer(1, d), per_layer(1, d),
            per_layer(d, d),
            _const_spec((1, d)), _const_spec((d, d)), _const_spec((d, d)), _const_spec((n_heads, d)),
            _const_spec((n_heads, 1)), _const_spec((HEAD_DIM, 1)),
        ],
        out_specs=[
            pl.BlockSpec((1, 1, n, d), lambda l, s: (l, jnp.maximum(s - 1, 0), 0, 0)),
            pl.BlockSpec((n, d), lambda l, s: (0, 0)),
            pl.BlockSpec((d, n), lambda l, s: (0, 0)),
            pl.BlockSpec((d, n), lambda l, s: (0, 0)),
            pl.BlockSpec((n_heads, n), lambda l, s: (0, 0)),
        ],
        out_shape=[
            jax.ShapeDtypeStruct((n_layers, n_hist, n, d), F32),
            jax.ShapeDtypeStruct((n, d), F32),
            jax.ShapeDtypeStruct((d, n), F32),
            jax.ShapeDtypeStruct((d, n), F32),
            jax.ShapeDtypeStruct((n_heads, n), F32),
        ],
        scratch_shapes=[pltpu.VMEM((n, d), F32)] * 4,
        compiler_params=_params("arbitrary", "arbitrary"),
        name="a_sample",
    )(x, st, cw4, g, w_in, cb, lg, lb, w_out, kvg, wk, wv, wft, fb_col, kg_col)


def _bias_sample_body(pt_ref, lfc_ref, lfst_ref, lower_ref, o_ref, *, n_pages, group):
    g = pl.program_id(0)
    pages = [[lfc_ref[pt_ref[g * group + r, j]] for j in range(n_pages)] for r in range(group)]
    n_heads, page = pages[0][0].shape
    stacked = jnp.concatenate([pg for per_sample in pages for pg in per_sample], axis=0)
    within = sum(jnp.dot(term.astype(BF16), lower_ref[...], preferred_element_type=F32)
                 for term in _split_bf16(stacked))
    lane = lax.broadcasted_iota(jnp.int32, lfst_ref.shape, 1)
    for r in range(group):
        carry = jnp.sum(jnp.where(lane == g * group + r, lfst_ref[...], 0.0), axis=1, keepdims=True)
        for j in reversed(range(n_pages)):
            row0 = (r * n_pages + j) * n_heads
            o_ref[r, :, j * page:(j + 1) * page] = within[row0:row0 + n_heads] + carry
            carry = carry + jnp.sum(pages[r][j], axis=1, keepdims=True)


def _bias_sample(page_table, lfc, lfst):
    n, n_pages = page_table.shape
    n_pool, n_heads, page = lfc.shape
    group = BIAS_GROUP
    lower = jnp.tril(jnp.ones((page, page), BF16), k=-1)
    return pl.pallas_call(
        functools.partial(_bias_sample_body, n_pages=n_pages, group=group),
        grid_spec=pltpu.PrefetchScalarGridSpec(
            num_scalar_prefetch=1,
            grid=(n // group,),
            in_specs=[_const_spec((n_pool, n_heads, page)), _const_spec((n_heads, n)), _const_spec((page, page))],
            out_specs=pl.BlockSpec((group, n_heads, n_pages * page), lambda i, pt: (i, 0, 0)),
        ),
        out_shape=jax.ShapeDtypeStruct((n, n_heads, n_pages * page), F32),
        compiler_params=_params("arbitrary"),
        name="bias_sample",
    )(page_table, lfc, lfst, lower)


def _sample_scores(b, q_row, bias_ref, kst_ref, k_refs, n_heads):
    d = q_row.shape[-1]
    n = kst_ref.shape[-1]
    own = (lax.broadcasted_iota(jnp.int32, (n_heads, d), 1) // HEAD_DIM
           == lax.broadcasted_iota(jnp.int32, (n_heads, d), 0))
    qbd = jnp.where(own, jnp.broadcast_to(q_row, (n_heads, d)), 0.0).astype(BF16)
    s = jnp.concatenate(
        [jnp.dot(qbd, k_ref[0].astype(BF16), preferred_element_type=F32) for k_ref in k_refs],
        axis=1) + bias_ref[0]
    is_b = lax.broadcasted_iota(jnp.int32, (n_heads, n), 1) == b
    k_self = jnp.where(lax.broadcasted_iota(jnp.int32, (d, n), 1) == b, kst_ref[...], 0.0).astype(BF16)
    s_self = jnp.where(is_b, jnp.dot(qbd, k_self, preferred_element_type=F32), NEG_BIG)
    m = jnp.maximum(jnp.max(s, axis=1, keepdims=True), jnp.max(s_self, axis=1, keepdims=True))
    p = jnp.exp(s - m)
    p_self = jnp.exp(s_self - m)
    l = jnp.sum(p, axis=1, keepdims=True) + jnp.sum(p_self, axis=1, keepdims=True)
    return p, p_self, l


def _weighted_values(p_pages, v_pages, h):
    rows = slice(h * HEAD_DIM, (h + 1) * HEAD_DIM)
    acc = None
    for pw, v in zip(p_pages, v_pages):
        term = v[rows, :] * pw[h:h + 1, :]
        acc = term if acc is None else acc + term
    return jnp.sum(acc, axis=1, keepdims=True)


def _ab_body(pt_ref, xs_ref, gb_ref, winb_ref, qg_ref, woutb_ref, bias_ref, kst_ref, vst_ref,
             xp_ref, ga_ref, wina_ref, cw_ref, cb_ref, lg_ref, lb_ref, wouta_ref, *rest,
             n_heads, n_pages, tiles_per_seq):
    k_refs = rest[:n_pages]
    v_refs = rest[n_pages:2 * n_pages]
    ys_ref, yp_ref, st_ref, q_sc, szs_sc, ot_sc, hist_ref, sza_ref, yc_ref = rest[2 * n_pages:]
    n, d = xs_ref.shape
    tt = xp_ref.shape[1]
    i = pl.program_id(0)
    t = i % tiles_per_seq

    @pl.when(i == 0)
    def _():
        xn = _rms_rows(xs_ref[...], gb_ref[...]).astype(BF16)
        u = jnp.dot(xn, winb_ref[...], preferred_element_type=F32)
        scale = 1.0 / math.sqrt(HEAD_DIM)
        q_sc[...] = (_head_rms_t(u[:, :d].T, qg_ref[...], n_heads) * scale).T
        szs_sc[...] = _silu(u[:, d:])
        ot_sc[...] = jnp.zeros_like(ot_sc)

    x = xp_ref[0]
    _a_tile_project(x, t == 0, ga_ref, wina_ref, hist_ref, sza_ref)
    p, p_self, l = _sample_scores(i, q_sc[pl.ds(i, 1), :], bias_ref, kst_ref, k_refs, n_heads)
    for piece in range((tt // CONV_ROWS) * (d // LANES)):
        _a_tile_conv_piece(piece, hist_ref, cw_ref, yc_ref)
    page = p_self.shape[-1]
    p_pages = [p_self] + [p[:, j * page:(j + 1) * page] for j in range(n_pages)]
    v_pages = [vst_ref] + [v_ref.at[0] for v_ref in v_refs]
    inv_l = 1.0 / l
    is_i = lax.broadcasted_iota(jnp.int32, (HEAD_DIM, n), 1) == i
    for h in range(n_heads):
        rows = slice(h * HEAD_DIM, (h + 1) * HEAD_DIM)
        oh = _weighted_values(p_pages, v_pages, h) * inv_l[h:h + 1, :]
        ot_sc[rows, :] = jnp.where(is_i, oh, ot_sc[rows, :])
    yp_ref[0] = _a_tile_finish(x, cb_ref, lg_ref, lb_ref, wouta_ref, sza_ref, yc_ref)

    @pl.when(t == tiles_per_seq - 1)
    def _():
        st_ref[0] = hist_ref[tt + CONV_HALO - (CONV_WIDTH - 1):tt + CONV_HALO, :]

    hist_ref[0:CONV_HALO, :] = hist_ref[tt:tt + CONV_HALO, :]

    @pl.when(i == pl.num_programs(0) - 1)
    def _():
        mm = (ot_sc[...].T * szs_sc[...]).astype(BF16)
        ys_ref[...] = xs_ref[...] + jnp.dot(mm, woutb_ref[...], preferred_element_type=F32)


def _ab_layer(page_table, xs, gb, w_in_b, qg_col, w_out_b, bias, kst, vst, ckt, cvt,
              xp, ga, w_in_a, cw, cb, lg, lb, w_out_a):
    n, d = xs.shape
    bp, tp, _ = xp.shape
    n_pages = page_table.shape[1]
    n_heads = bias.shape[1]
    page = ckt.shape[-1]
    tt = A_TILE
    tiles_per_seq = tp // tt
    assert page == n, "the new token's key rides through the kernel as one more page-sized block"
    assert bp * tiles_per_seq == n, "one prompt tile per sample grid step"
    page_spec = lambda j: pl.BlockSpec((1, d, page), lambda i, pt, j=j: (pt[i, j], 0, 0))
    tile_spec = pl.BlockSpec((1, tt, d), lambda i, pt: (i // tiles_per_seq, i % tiles_per_seq, 0))
    row = lambda: _const_spec((1, d))
    return pl.pallas_call(
        functools.partial(_ab_body, n_heads=n_heads, n_pages=n_pages, tiles_per_seq=tiles_per_seq),
        grid_spec=pltpu.PrefetchScalarGridSpec(
            num_scalar_prefetch=1,
            grid=(n,),
            in_specs=[
                _const_spec((n, d)), row(), _const_spec((d, 2 * d)), _const_spec((HEAD_DIM, 1)),
                _const_spec((d, d)),
                pl.BlockSpec((1, n_heads, n_pages * page), lambda i, pt: (i, 0, 0)),
                _const_spec((d, n)), _const_spec((d, n)),
                tile_spec, row(), _const_spec((d, 3 * d)), _const_spec((CONV_WIDTH, d)), row(), row(), row(),
                _const_spec((d, d)),
            ] + [page_spec(j) for j in range(n_pages)] * 2,
            out_specs=[
                pl.BlockSpec((n, d), lambda i, pt: (0, 0)),
                tile_spec,
                pl.BlockSpec((1, CONV_WIDTH - 1, d), lambda i, pt: (i // tiles_per_seq, 0, 0)),
            ],
            scratch_shapes=[pltpu.VMEM((n, d), F32)] * 2 + [pltpu.VMEM((d, n), F32)] + [
                pltpu.VMEM((tt + CONV_HALO, d), F32), pltpu.VMEM((tt, d), F32), pltpu.VMEM((tt, d), F32)],
        ),
        out_shape=[
            jax.ShapeDtypeStruct((n, d), F32),
            jax.ShapeDtypeStruct((bp, tp, d), F32),
            jax.ShapeDtypeStruct((bp, CONV_WIDTH - 1, d), F32),
        ],
        compiler_params=_params("arbitrary"),
        name="ab_layer",
    )(page_table, xs, gb, w_in_b, qg_col, w_out_b, bias, kst, vst,
      xp, ga, w_in_a, cw, cb, lg, lb, w_out_a, *([ckt] * n_pages), *([cvt] * n_pages))


def kernel(x_prompt, x_sample, state_conv, cache_k, cache_v, cache_logf, page_table, a_norm, a_w_in, a_conv_w,
           a_conv_b, a_ln_g, a_ln_b, a_w_out, kv_norm, kv_w, kv_fb, k_norm, b_norm, b_w_in, q_norm, b_w_out):
    bp, tp, d = x_prompt.shape
    bs = x_sample.shape[0]
    n_a = a_w_in.shape[0]
    n_b = b_w_in.shape[0]
    n_heads = kv_fb.shape[0]
    attn = n_heads * HEAD_DIM
    n_pool, page = cache_k.shape[0], cache_k.shape[1]

    a_w_in_b = a_w_in.astype(BF16)
    a_w_out_b = a_w_out.astype(BF16)
    b_w_in_b = b_w_in.astype(BF16)
    b_w_out_b = b_w_out.astype(BF16)
    wk = kv_w[:, :attn].astype(BF16)
    wv = kv_w[:, attn:2 * attn].astype(BF16)
    wft = kv_w[:, 2 * attn:].T.astype(BF16)
    fb_col = kv_fb.reshape(n_heads, 1)
    kg_col = k_norm.reshape(HEAD_DIM, 1)
    kvg = kv_norm.reshape(1, d)

    st_t = jnp.transpose(state_conv, (0, 2, 1, 3))
    nst_t, xs, kst, vst, lfst = _a_sample(
        x_sample.reshape(bs, d), st_t, a_conv_w[:, :, None, :], a_norm[:, None], a_w_in_b, a_conv_b[:, None],
        a_ln_g[:, None], a_ln_b[:, None], a_w_out_b, kvg, wk, wv, wft, fb_col, kg_col)
    ckt = jnp.transpose(cache_k, (0, 2, 3, 1)).reshape(n_pool, attn, page)
    cvt = jnp.transpose(cache_v, (0, 2, 3, 1)).reshape(n_pool, attn, page)
    lfc = jnp.transpose(cache_logf, (0, 2, 1))
    bias = _bias_sample(page_table, lfc, lfst)

    assert n_a == n_b
    xp = x_prompt
    conv_p = []
    for j in range(n_b):
        xs, xp, st = _ab_layer(
            page_table, xs, b_norm[j][None], b_w_in_b[j], q_norm[j].reshape(HEAD_DIM, 1), b_w_out_b[j],
            bias, kst, vst, ckt, cvt,
            xp, a_norm[j][None], a_w_in_b[j], a_conv_w[j], a_conv_b[j][None], a_ln_g[j][None], a_ln_b[j][None],
            a_w_out_b[j])
        conv_p.append(st)

    kt, vt, lft, kb, vtb, ct = _kv_prompt(xp, kvg, wk, wv, wft, fb_col, kg_col)
    for j in range(n_b):
        xp = _b_prompt(xp, b_norm[j][None], b_w_in_b[j], q_norm[j].reshape(HEAD_DIM, 1), b_w_out_b[j],
                       kb, vtb, ct)

    heads_last = lambda a: jnp.transpose(a.reshape(a.shape[0], n_heads, HEAD_DIM, a.shape[-1]), (0, 3, 1, 2))
    return (
        xp,
        xs.reshape(bs, 1, d),
        jnp.stack(conv_p, axis=0),
        jnp.transpose(nst_t, (0, 2, 1, 3)),
        heads_last(kt),
        heads_last(vt),
        jnp.transpose(lft, (0, 2, 1)),
        jnp.transpose(kst.reshape(n_heads, HEAD_DIM, bs), (2, 0, 1))[:, None],
        jnp.transpose(vst.reshape(n_heads, HEAD_DIM, bs), (2, 0, 1))[:, None],
        jnp.transpose(lfst, (1, 0))[:, None],
    )
```

```python
import functools
import math

import jax
import jax.numpy as jnp
from jax import lax
from jax.experimental import pallas as pl
from jax.experimental.pallas import tpu as pltpu

F32 = jnp.float32
BF16 = jnp.bfloat16
EPS = 1e-6
HEAD_DIM = 64
CONV_WIDTH = 31
CONV_HALO = 32
SUBLANES = 8
LANES = 128
NEG_BIG = -1e30
V7X_VMEM_LIMIT_BYTES = 56 * 2**20

A_TILE = 128
CONV_ROWS = 128
KV_TILE = 256
Q_TILE = 256
LOG2E = 1.4426950408889634
N_SPLIT = 3
BIAS_GROUP = 8
ONES_ROWS = 2 * SUBLANES


def _rms_rows(x, g):
    return x * lax.rsqrt(jnp.mean(x * x, axis=-1, keepdims=True) + EPS) * g


def _silu(x):
    return x * jax.nn.sigmoid(x)


def _log_sigmoid(x):
    return -(jnp.maximum(-x, 0.0) + jnp.log1p(jnp.exp(-jnp.abs(x))))


def _layernorm_rows(x, g, b):
    xc = x - jnp.mean(x, axis=-1, keepdims=True)
    return xc * lax.rsqrt(jnp.mean(xc * xc, axis=-1, keepdims=True) + EPS) * g + b


def _head_rms_t(xt, g_col, n_heads):
    n = xt.shape[-1]
    x3 = xt.reshape(n_heads, HEAD_DIM, n)
    ms = jnp.mean(x3 * x3, axis=1, keepdims=True)
    return (x3 * lax.rsqrt(ms + EPS) * g_col[None]).reshape(n_heads * HEAD_DIM, n)


def _const_spec(shape):
    nd = len(shape)
    return pl.BlockSpec(shape, lambda *_: (0,) * nd, pipeline_mode=pl.Buffered(1))


def _params(*semantics):
    return pltpu.CompilerParams(dimension_semantics=semantics, vmem_limit_bytes=V7X_VMEM_LIMIT_BYTES)


def _conv_chunk(hist_ref, cw_ref, r0, rows, lsl):
    acc = None
    for phase in range(SUBLANES):
        offs = [o for o in range(2, CONV_HALO + 1) if o % SUBLANES == phase]
        n = rows if phase == 0 else rows + SUBLANES
        part = None
        for o in offs:
            term = cw_ref[o - 2:o - 1, lsl] * hist_ref[pl.ds(r0 + (o - phase), n), lsl]
            part = term if part is None else part + term
        part = part if phase == 0 else part[phase:phase + rows]
        acc = part if acc is None else acc + part
    return acc


def _a_tile_project(x, first, g_ref, win_ref, hist_ref, sz_ref):
    tt, d = x.shape

    @pl.when(first)
    def _():
        hist_ref[0:CONV_HALO, :] = jnp.zeros((CONV_HALO, d), F32)

    xn = _rms_rows(x, g_ref[...]).astype(BF16)
    u = jnp.dot(xn, win_ref[...], preferred_element_type=F32)
    hist_ref[CONV_HALO:CONV_HALO + tt, :] = u[:, :d] * jax.nn.sigmoid(u[:, d:2 * d])
    sz_ref[...] = _silu(u[:, 2 * d:])


def _a_tile_conv_piece(piece, hist_ref, cw_ref, yc_ref):
    n_lane_blocks = yc_ref.shape[1] // LANES
    c, l = divmod(piece, n_lane_blocks)
    lsl = slice(l * LANES, (l + 1) * LANES)
    yc_ref[c * CONV_ROWS:(c + 1) * CONV_ROWS, lsl] = _conv_chunk(hist_ref, cw_ref, c * CONV_ROWS, CONV_ROWS, lsl)


def _a_tile_finish(x, cb_ref, lg_ref, lb_ref, wout_ref, sz_ref, yc_ref):
    y = _silu(_layernorm_rows(yc_ref[...] + cb_ref[...], lg_ref[...], lb_ref[...]))
    m = (y * sz_ref[...]).astype(BF16)
    return x + jnp.dot(m, wout_ref[...], preferred_element_type=F32)


def _split_bf16(x):
    parts = []
    for _ in range(N_SPLIT - 1):
        hi = x.astype(BF16).astype(F32)
        parts.append(hi)
        x = x - hi
    return parts + [x]


def _kv_prompt_body(x_ref, g_ref, wk_ref, wv_ref, wft_ref, fb_ref, kg_ref, triu_ref,
                    kt_ref, vt_ref, lft_ref, kb_ref, vtb_ref, ct_ref, carry_ref, *, n_heads):
    t = pl.program_id(1)

    @pl.when(t == 0)
    def _():
        carry_ref[...] = jnp.zeros_like(carry_ref)

    xn = _rms_rows(x_ref[0], g_ref[...]).astype(BF16)
    tt = xn.shape[0]
    k = jnp.dot(xn, wk_ref[...], preferred_element_type=F32)
    v = jnp.dot(xn, wv_ref[...], preferred_element_type=F32)
    knt = _head_rms_t(k.T, kg_ref[...], n_heads)
    vt = v.T
    kt_ref[0] = knt
    vt_ref[0] = vt
    vtb_ref[0, 0] = vt.astype(BF16)

    uft = lax.dot_general(wft_ref[...], xn, (((1,), (1,)), ((), ())), preferred_element_type=F32)
    lft = _log_sigmoid(uft + fb_ref[...])
    lft_ref[0] = lft
    ct = jnp.dot(lft, triu_ref[...], precision=lax.Precision.HIGHEST,
                 preferred_element_type=F32) + carry_ref[...]
    ct_ref[0] = ct
    carry_ref[...] = ct[:, tt - 1:tt]

    terms = _split_bf16(ct * (-LOG2E))
    row = lax.broadcasted_iota(jnp.int32, (SUBLANES, tt), 0)
    pad = jnp.zeros((HEAD_DIM - SUBLANES, tt), F32)
    blocks = []
    for h in range(n_heads):
        aug = jnp.zeros((SUBLANES, tt), F32)
        for i, term in enumerate(terms):
            aug = jnp.where(row == i, term[h:h + 1, :], aug)
        blocks += [knt[h * HEAD_DIM:(h + 1) * HEAD_DIM], aug, pad]
    kb_ref[0] = jnp.concatenate(blocks, axis=0).T.astype(BF16)


def _kv_prompt(x, g, wk, wv, wft, fb_col, kg_col):
    b, t, d = x.shape
    n_heads = wft.shape[0]
    tt = KV_TILE
    triu = jnp.triu(jnp.ones((tt, tt), F32))
    feat_major = lambda rows: pl.BlockSpec((1, rows, tt), lambda i, j: (i, 0, j))
    return pl.pallas_call(
        functools.partial(_kv_prompt_body, n_heads=n_heads),
        grid=(b, t // tt),
        in_specs=[
            pl.BlockSpec((1, tt, d), lambda i, j: (i, j, 0)),
            _const_spec((1, d)), _const_spec((d, d)), _const_spec((d, d)), _const_spec((n_heads, d)),
            _const_spec((n_heads, 1)), _const_spec((HEAD_DIM, 1)), _const_spec((tt, tt)),
        ],
        out_specs=[
            feat_major(d), feat_major(d), feat_major(n_heads),
            pl.BlockSpec((1, tt, 2 * d), lambda i, j: (i, j, 0)),
            pl.BlockSpec((1, 1, d, tt), lambda i, j: (i, j, 0, 0)),
            feat_major(n_heads),
        ],
        out_shape=[
            jax.ShapeDtypeStruct((b, d, t), F32),
            jax.ShapeDtypeStruct((b, d, t), F32),
            jax.ShapeDtypeStruct((b, n_heads, t), F32),
            jax.ShapeDtypeStruct((b, t, 2 * d), BF16),
            jax.ShapeDtypeStruct((b, t // tt, d, tt), BF16),
            jax.ShapeDtypeStruct((b, n_heads, t), F32),
        ],
        scratch_shapes=[pltpu.VMEM((n_heads, 1), F32)],
        compiler_params=_params("arbitrary", "arbitrary"),
        name="kv_prompt",
    )(x, g, wk, wv, wft, fb_col, kg_col, triu)


def _b_prompt_body(x_ref, g_ref, win_ref, qg_ref, wout_ref, kb_ref, vtb_ref, ctq_ref,
                   o_ref, qm_ref, ot_ref, sz_ref, m_ref, acc_ref, *, n_heads, tq, tk):
    d = x_ref.shape[-1]
    i = pl.program_id(1)
    x = x_ref[0]
    xn = _rms_rows(x, g_ref[...]).astype(BF16)
    u = jnp.dot(xn, win_ref[...], preferred_element_type=F32)
    sz_ref[...] = _silu(u[:, d:])
    scale = LOG2E / math.sqrt(HEAD_DIM)
    qnt = (_head_rms_t(u[:, :d].T, qg_ref[...], n_heads) * scale).astype(BF16)
    ones = (lax.broadcasted_iota(jnp.int32, (HEAD_DIM, tq), 0) < N_SPLIT).astype(BF16)
    for h in range(n_heads):
        qm_ref[h, 0:HEAD_DIM, :] = qnt[h * HEAD_DIM:(h + 1) * HEAD_DIM]
        qm_ref[h, HEAD_DIM:2 * HEAD_DIM, :] = ones
        m_ref[h] = jnp.full((1, tq), NEG_BIG, F32)
        acc_ref[h] = jnp.zeros((HEAD_DIM + ONES_ROWS, tq), F32)

    causal = (lax.broadcasted_iota(jnp.int32, (tk, tq), 0) <= lax.broadcasted_iota(jnp.int32, (tk, tq), 1))
    ones_rows = jnp.ones((ONES_ROWS, tk), BF16)

    def step(j, carry, diagonal):
        k0 = pl.multiple_of(j * tk, tk)
        sts = [jnp.dot(kb_ref[0, pl.ds(k0, tk), h * 2 * HEAD_DIM:(h + 1) * 2 * HEAD_DIM], qm_ref[h],
                       preferred_element_type=F32) for h in range(n_heads)]
        ps, alphas = [], []
        for h, st in enumerate(sts):
            cq = ctq_ref[0, h:h + 1, :] * LOG2E
            if diagonal:
                st = jnp.where(causal, st, NEG_BIG)
            m = m_ref[h]
            m_new = jnp.maximum(m, jnp.max(st, axis=0, keepdims=True) + cq)
            alphas.append(jnp.exp2(m - m_new))
            ps.append(jnp.exp2(st - (m_new - cq)).astype(BF16))
            m_ref[h] = m_new
        for h, (p, alpha) in enumerate(zip(ps, alphas)):
            vh = jnp.concatenate([vtb_ref[0, j, h * HEAD_DIM:(h + 1) * HEAD_DIM, :], ones_rows], axis=0)
            acc_ref[h] = alpha * acc_ref[h] + jnp.dot(vh, p, preferred_element_type=F32)
        return carry

    lax.fori_loop(0, i, functools.partial(step, diagonal=False), 0)
    step(i, 0, True)
    for h in range(n_heads):
        ot_ref[h * HEAD_DIM:(h + 1) * HEAD_DIM, :] = acc_ref[h, 0:HEAD_DIM, :] / acc_ref[h, HEAD_DIM:HEAD_DIM + 1, :]

    m = (ot_ref[...].T * sz_ref[...]).astype(BF16)
    o_ref[0] = x + jnp.dot(m, wout_ref[...], preferred_element_type=F32)


def _b_prompt(x, g, w_in, qg_col, w_out, kb, vtb, ct):
    b, t, d = x.shape
    n_heads = ct.shape[1]
    tq, tk = Q_TILE, KV_TILE
    assert tq == tk and vtb.shape == (b, t // tk, d, tk)
    return pl.pallas_call(
        functools.partial(_b_prompt_body, n_heads=n_heads, tq=tq, tk=tk),
        grid=(b, t // tq),
        in_specs=[
            pl.BlockSpec((1, tq, d), lambda i, j: (i, j, 0)),
            _const_spec((1, d)), _const_spec((d, 2 * d)), _const_spec((HEAD_DIM, 1)), _const_spec((d, d)),
            pl.BlockSpec((1, t, 2 * d), lambda i, j: (i, 0, 0)),
            pl.BlockSpec((1, t // tk, d, tk), lambda i, j: (i, 0, 0, 0)),
            pl.BlockSpec((1, n_heads, tq), lambda i, j: (i, 0, j)),
        ],
        out_specs=pl.BlockSpec((1, tq, d), lambda i, j: (i, j, 0)),
        out_shape=jax.ShapeDtypeStruct((b, t, d), F32),
        scratch_shapes=[
            pltpu.VMEM((n_heads, 2 * HEAD_DIM, tq), BF16),
            pltpu.VMEM((d, tq), F32),
            pltpu.VMEM((tq, d), F32),
            pltpu.VMEM((n_heads, 1, tq), F32),
            pltpu.VMEM((n_heads, HEAD_DIM + ONES_ROWS, tq), F32),
        ],
        compiler_params=_params("arbitrary", "arbitrary"),
        name="b_prompt",
    )(x, g, w_in, qg_col, w_out, kb, vtb, ct)


def _a_sample_body(x_ref, st_ref, cw_ref, g_ref, win_ref, cb_ref, lg_ref, lb_ref, wout_ref,
                   kvg_ref, wk_ref, wv_ref, wft_ref, fb_ref, kg_ref,
                   nst_ref, xo_ref, kst_ref, vst_ref, lfst_ref,
                   x_sc, glu_sc, sz_sc, acc_sc, *, n_heads):
    d = x_ref.shape[-1]
    layer = pl.program_id(0)
    s = pl.program_id(1)
    n_hist = CONV_WIDTH - 1

    @pl.when((layer == 0) & (s == 0))
    def _():
        x_sc[...] = x_ref[...]

    @pl.when(s == 0)
    def _():
        xn = _rms_rows(x_sc[...], g_ref[0]).astype(BF16)
        u = jnp.dot(xn, win_ref[0], preferred_element_type=F32)
        glu_sc[...] = u[:, :d] * jax.nn.sigmoid(u[:, d:2 * d])
        sz_sc[...] = _silu(u[:, 2 * d:])
        acc_sc[...] = jnp.zeros_like(acc_sc)

    @pl.when(s < n_hist)
    def _():
        acc_sc[...] += cw_ref[0, 0] * st_ref[0, 0]

    @pl.when((s >= 1) & (s < n_hist))
    def _():
        nst_ref[0, 0] = st_ref[0, 0]

    @pl.when(s == n_hist)
    def _():
        glu = glu_sc[...]
        nst_ref[0, 0] = glu
        yc = acc_sc[...] + cw_ref[0, 0] * glu + cb_ref[0]
        y = _silu(_layernorm_rows(yc, lg_ref[0], lb_ref[0]))
        m = (y * sz_sc[...]).astype(BF16)
        x_sc[...] = x_sc[...] + jnp.dot(m, wout_ref[0], preferred_element_type=F32)

    @pl.when((s == n_hist) & (layer == pl.num_programs(0) - 1))
    def _():
        x = x_sc[...]
        xo_ref[...] = x
        xn = _rms_rows(x, kvg_ref[...]).astype(BF16)
        k = jnp.dot(xn, wk_ref[...], preferred_element_type=F32)
        v = jnp.dot(xn, wv_ref[...], preferred_element_type=F32)
        kst_ref[...] = _head_rms_t(k.T, kg_ref[...], n_heads)
        vst_ref[...] = v.T
        uft = lax.dot_general(wft_ref[...], xn, (((1,), (1,)), ((), ())), preferred_element_type=F32)
        lfst_ref[...] = _log_sigmoid(uft + fb_ref[...])


def _a_sample(x, st, cw4, g, w_in, cb, lg, lb, w_out, kvg, wk, wv, wft, fb_col, kg_col):
    n, d = x.shape
    n_layers, n_hist = st.shape[0], st.shape[1]
    n_heads = wft.shape[0]
    per_layer = lambda *shape: pl.BlockSpec((1,) + shape, lambda l, s: (l,) + (0,) * len(shape))
    return pl.pallas_call(
        functools.partial(_a_sample_body, n_heads=n_heads),
        grid=(n_layers, n_hist + 1),
        in_specs=[
            _const_spec((n, d)),
            pl.BlockSpec((1, 1, n, d), lambda l, s: (l, jnp.minimum(s, n_hist - 1), 0, 0)),
            pl.BlockSpec((1, 1, 1, d), lambda l, s: (l, s, 0, 0)),
            per_layer(1, d), per_layer(d, 3 * d), per_layer(1, d), per_layer(1, d), per_layer(1, d),
            per_layer(d, d),
            _const_spec((1, d)), _const_spec((d, d)), _const_spec((d, d)), _const_spec((n_heads, d)),
            _const_spec((n_heads, 1)), _const_spec((HEAD_DIM, 1)),
        ],
        out_specs=[
            pl.BlockSpec((1, 1, n, d), lambda l, s: (l, jnp.maximum(s - 1, 0), 0, 0)),
            pl.BlockSpec((n, d), lambda l, s: (0, 0)),
            pl.BlockSpec((d, n), lambda l, s: (0, 0)),
            pl.BlockSpec((d, n), lambda l, s: (0, 0)),
            pl.BlockSpec((n_heads, n), lambda l, s: (0, 0)),
        ],
        out_shape=[
            jax.ShapeDtypeStruct((n_layers, n_hist, n, d), F32),
            jax.ShapeDtypeStruct((n, d), F32),
            jax.ShapeDtypeStruct((d, n), F32),
            jax.ShapeDtypeStruct((d, n), F32),
            jax.ShapeDtypeStruct((n_heads, n), F32),
        ],
        scratch_shapes=[pltpu.VMEM((n, d), F32)] * 4,
        compiler_params=_params("arbitrary", "arbitrary"),
        name="a_sample",
    )(x, st, cw4, g, w_in, cb, lg, lb, w_out, kvg, wk, wv, wft, fb_col, kg_col)


def _bias_sample_body(pt_ref, lfc_ref, lfst_ref, lower_ref, o_ref, *, n_pages, group):
    g = pl.program_id(0)
    pages = [[lfc_ref[pt_ref[g * group + r, j]] for j in range(n_pages)] for r in range(group)]
    n_heads, page = pages[0][0].shape
    stacked = jnp.concatenate([pg for per_sample in pages for pg in per_sample], axis=0)
    within = sum(jnp.dot(term.astype(BF16), lower_ref[...], preferred_element_type=F32)
                 for term in _split_bf16(stacked))
    lane = lax.broadcasted_iota(jnp.int32, lfst_ref.shape, 1)
    for r in range(group):
        carry = jnp.sum(jnp.where(lane == g * group + r, lfst_ref[...], 0.0), axis=1, keepdims=True)
        for j in reversed(range(n_pages)):
            row0 = (r * n_pages + j) * n_heads
            o_ref[r, :, j * page:(j + 1) * page] = within[row0:row0 + n_heads] + carry
            carry = carry + jnp.sum(pages[r][j], axis=1, keepdims=True)


def _bias_sample(page_table, lfc, lfst):
    n, n_pages = page_table.shape
    n_pool, n_heads, page = lfc.shape
    group = BIAS_GROUP
    lower = jnp.tril(jnp.ones((page, page), BF16), k=-1)
    return pl.pallas_call(
        functools.partial(_bias_sample_body, n_pages=n_pages, group=group),
        grid_spec=pltpu.PrefetchScalarGridSpec(
            num_scalar_prefetch=1,
            grid=(n // group,),
            in_specs=[_const_spec((n_pool, n_heads, page)), _const_spec((n_heads, n)), _const_spec((page, page))],
            out_specs=pl.BlockSpec((group, n_heads, n_pages * page), lambda i, pt: (i, 0, 0)),
        ),
        out_shape=jax.ShapeDtypeStruct((n, n_heads, n_pages * page), F32),
        compiler_params=_params("arbitrary"),
        name="bias_sample",
    )(page_table, lfc, lfst, lower)


def _sample_scores(b, q_row, bias_ref, kst_ref, k_refs, n_heads):
    d = q_row.shape[-1]
    n = kst_ref.shape[-1]
    own = (lax.broadcasted_iota(jnp.int32, (n_heads, d), 1) // HEAD_DIM
           == lax.broadcasted_iota(jnp.int32, (n_heads, d), 0))
    qbd = jnp.where(own, jnp.broadcast_to(q_row, (n_heads, d)), 0.0).astype(BF16)
    s = jnp.concatenate(
        [jnp.dot(qbd, k_ref[0].astype(BF16), preferred_element_type=F32) for k_ref in k_refs],
        axis=1) + bias_ref[0]
    is_b = lax.broadcasted_iota(jnp.int32, (n_heads, n), 1) == b
    k_self = jnp.where(lax.broadcasted_iota(jnp.int32, (d, n), 1) == b, kst_ref[...], 0.0).astype(BF16)
    s_self = jnp.where(is_b, jnp.dot(qbd, k_self, preferred_element_type=F32), NEG_BIG)
    m = jnp.maximum(jnp.max(s, axis=1, keepdims=True), jnp.max(s_self, axis=1, keepdims=True))
    p = jnp.exp(s - m)
    p_self = jnp.exp(s_self - m)
    l = jnp.sum(p, axis=1, keepdims=True) + jnp.sum(p_self, axis=1, keepdims=True)
    return p, p_self, l


def _weighted_values(p_pages, v_pages, h):
    rows = slice(h * HEAD_DIM, (h + 1) * HEAD_DIM)
    acc = None
    for pw, v in zip(p_pages, v_pages):
        term = v[rows, :] * pw[h:h + 1, :]
        acc = term if acc is None else acc + term
    return jnp.sum(acc, axis=1, keepdims=True)


def _ab_body(pt_ref, xs_ref, gb_ref, winb_ref, qg_ref, woutb_ref, bias_ref, kst_ref, vst_ref,
             xp_ref, ga_ref, wina_ref, cw_ref, cb_ref, lg_ref, lb_ref, wouta_ref, *rest,
             n_heads, n_pages, tiles_per_seq):
    k_refs = rest[:n_pages]
    v_refs = rest[n_pages:2 * n_pages]
    ys_ref, yp_ref, st_ref, q_sc, szs_sc, ot_sc, hist_ref, sza_ref, yc_ref = rest[2 * n_pages:]
    n, d = xs_ref.shape
    tt = xp_ref.shape[1]
    i = pl.program_id(0)
    t = i % tiles_per_seq

    @pl.when(i == 0)
    def _():
        xn = _rms_rows(xs_ref[...], gb_ref[...]).astype(BF16)
        u = jnp.dot(xn, winb_ref[...], preferred_element_type=F32)
        scale = 1.0 / math.sqrt(HEAD_DIM)
        q_sc[...] = (_head_rms_t(u[:, :d].T, qg_ref[...], n_heads) * scale).T
        szs_sc[...] = _silu(u[:, d:])
        ot_sc[...] = jnp.zeros_like(ot_sc)

    x = xp_ref[0]
    _a_tile_project(x, t == 0, ga_ref, wina_ref, hist_ref, sza_ref)
    p, p_self, l = _sample_scores(i, q_sc[pl.ds(i, 1), :], bias_ref, kst_ref, k_refs, n_heads)
    for piece in range((tt // CONV_ROWS) * (d // LANES)):
        _a_tile_conv_piece(piece, hist_ref, cw_ref, yc_ref)
    page = p_self.shape[-1]
    p_pages = [p_self] + [p[:, j * page:(j + 1) * page] for j in range(n_pages)]
    v_pages = [vst_ref] + [v_ref.at[0] for v_ref in v_refs]
    inv_l = 1.0 / l
    is_i = lax.broadcasted_iota(jnp.int32, (HEAD_DIM, n), 1) == i
    for h in range(n_heads):
        rows = slice(h * HEAD_DIM, (h + 1) * HEAD_DIM)
        oh = _weighted_values(p_pages, v_pages, h) * inv_l[h:h + 1, :]
        ot_sc[rows, :] = jnp.where(is_i, oh, ot_sc[rows, :])
    yp_ref[0] = _a_tile_finish(x, cb_ref, lg_ref, lb_ref, wouta_ref, sza_ref, yc_ref)

    @pl.when(t == tiles_per_seq - 1)
    def _():
        st_ref[0] = hist_ref[tt + CONV_HALO - (CONV_WIDTH - 1):tt + CONV_HALO, :]

    hist_ref[0:CONV_HALO, :] = hist_ref[tt:tt + CONV_HALO, :]

    @pl.when(i == pl.num_programs(0) - 1)
    def _():
        mm = (ot_sc[...].T * szs_sc[...]).astype(BF16)
        ys_ref[...] = xs_ref[...] + jnp.dot(mm, woutb_ref[...], preferred_element_type=F32)


def _ab_layer(page_table, xs, gb, w_in_b, qg_col, w_out_b, bias, kst, vst, ckt, cvt,
              xp, ga, w_in_a, cw, cb, lg, lb, w_out_a):
    n, d = xs.shape
    bp, tp, _ = xp.shape
    n_pages = page_table.shape[1]
    n_heads = bias.shape[1]
    page = ckt.shape[-1]
    tt = A_TILE
    tiles_per_seq = tp // tt
    assert page == n, "the new token's key rides through the kernel as one more page-sized block"
    assert bp * tiles_per_seq == n, "one prompt tile per sample grid step"
    page_spec = lambda j: pl.BlockSpec((1, d, page), lambda i, pt, j=j: (pt[i, j], 0, 0))
    tile_spec = pl.BlockSpec((1, tt, d), lambda i, pt: (i // tiles_per_seq, i % tiles_per_seq, 0))
    row = lambda: _const_spec((1, d))
    return pl.pallas_call(
        functools.partial(_ab_body, n_heads=n_heads, n_pages=n_pages, tiles_per_seq=tiles_per_seq),
        grid_spec=pltpu.PrefetchScalarGridSpec(
            num_scalar_prefetch=1,
            grid=(n,),
            in_specs=[
                _const_spec((n, d)), row(), _const_spec((d, 2 * d)), _const_spec((HEAD_DIM, 1)),
                _const_spec((d, d)),
                pl.BlockSpec((1, n_heads, n_pages * page), lambda i, pt: (i, 0, 0)),
                _const_spec((d, n)), _const_spec((d, n)),
                tile_spec, row(), _const_spec((d, 3 * d)), _const_spec((CONV_WIDTH, d)), row(), row(), row(),
                _const_spec((d, d)),
            ] + [page_spec(j) for j in range(n_pages)] * 2,
            out_specs=[
                pl.BlockSpec((n, d), lambda i, pt: (0, 0)),
                tile_spec,
                pl.BlockSpec((1, CONV_WIDTH - 1, d), lambda i, pt: (i // tiles_per_seq, 0, 0)),
            ],
            scratch_shapes=[pltpu.VMEM((n, d), F32)] * 2 + [pltpu.VMEM((d, n), F32)] + [
                pltpu.VMEM((tt + CONV_HALO, d), F32), pltpu.VMEM((tt, d), F32), pltpu.VMEM((tt, d), F32)],
        ),
        out_shape=[
            jax.ShapeDtypeStruct((n, d), F32),
            jax.ShapeDtypeStruct((bp, tp, d), F32),
            jax.ShapeDtypeStruct((bp, CONV_WIDTH - 1, d), F32),
        ],
        compiler_params=_params("arbitrary"),
        name="ab_layer",
    )(page_table, xs, gb, w_in_b, qg_col, w_out_b, bias, kst, vst,
      xp, ga, w_in_a, cw, cb, lg, lb, w_out_a, *([ckt] * n_pages), *([cvt] * n_pages))


def kernel(x_prompt, x_sample, state_conv, cache_k, cache_v, cache_logf, page_table, a_norm, a_w_in, a_conv_w,
           a_conv_b, a_ln_g, a_ln_b, a_w_out, kv_norm, kv_w, kv_fb, k_norm, b_norm, b_w_in, q_norm, b_w_out):
    bp, tp, d = x_prompt.shape
    bs = x_sample.shape[0]
    n_a = a_w_in.shape[0]
    n_b = b_w_in.shape[0]
    n_heads = kv_fb.shape[0]
    attn = n_heads * HEAD_DIM
    n_pool, page = cache_k.shape[0], cache_k.shape[1]

    a_w_in_b = a_w_in.astype(BF16)
    a_w_out_b = a_w_out.astype(BF16)
    b_w_in_b = b_w_in.astype(BF16)
    b_w_out_b = b_w_out.astype(BF16)
    wk = kv_w[:, :attn].astype(BF16)
    wv = kv_w[:, attn:2 * attn].astype(BF16)
    wft = kv_w[:, 2 * attn:].T.astype(BF16)
    fb_col = kv_fb.reshape(n_heads, 1)
    kg_col = k_norm.reshape(HEAD_DIM, 1)
    kvg = kv_norm.reshape(1, d)

    st_t = jnp.transpose(state_conv, (0, 2, 1, 3))
    nst_t, xs, kst, vst, lfst = _a_sample(
        x_sample.reshape(bs, d), st_t, a_conv_w[:, :, None, :], a_norm[:, None], a_w_in_b, a_conv_b[:, None],
        a_ln_g[:, None], a_ln_b[:, None], a_w_out_b, kvg, wk, wv, wft, fb_col, kg_col)
    ckt = jnp.transpose(cache_k, (0, 2, 3, 1)).reshape(n_pool, attn, page)
    cvt = jnp.transpose(cache_v, (0, 2, 3, 1)).reshape(n_pool, attn, page)
    lfc = jnp.transpose(cache_logf, (0, 2, 1))
    bias = _bias_sample(page_table, lfc, lfst)

    assert n_a == n_b
    xp = x_prompt
    conv_p = []
    for j in range(n_b):
        xs, xp, st = _ab_layer(
            page_table, xs, b_norm[j][None], b_w_in_b[j], q_norm[j].reshape(HEAD_DIM, 1), b_w_out_b[j],
            bias, kst, vst, ckt, cvt,
            xp, a_norm[j][None], a_w_in_b[j], a_conv_w[j], a_conv_b[j][None], a_ln_g[j][None], a_ln_b[j][None],
            a_w_out_b[j])
        conv_p.append(st)

    kt, vt, lft, kb, vtb, ct = _kv_prompt(xp, kvg, wk, wv, wft, fb_col, kg_col)
    for j in range(n_b):
        xp = _b_prompt(xp, b_norm[j][None], b_w_in_b[j], q_norm[j].reshape(HEAD_DIM, 1), b_w_out_b[j],
                       kb, vtb, ct)

    heads_last = lambda a: jnp.transpose(a.reshape(a.shape[0], n_heads, HEAD_DIM, a.shape[-1]), (0, 3, 1, 2))
    return (
        xp,
        xs.reshape(bs, 1, d),
        jnp.stack(conv_p, axis=0),
        jnp.transpose(nst_t, (0, 2, 1, 3)),
        heads_last(kt),
        heads_last(vt),
        jnp.transpose(lft, (0, 2, 1)),
        jnp.transpose(kst.reshape(n_heads, HEAD_DIM, bs), (2, 0, 1))[:, None],
        jnp.transpose(vst.reshape(n_heads, HEAD_DIM, bs), (2, 0, 1))[:, None],
        jnp.transpose(lfst, (1, 0))[:, None],
    )
```

```python
import functools
import math

import jax
import jax.numpy as jnp
from jax import lax
from jax.experimental import pallas as pl
from jax.experimental.pallas import tpu as pltpu

F32 = jnp.float32
BF16 = jnp.bfloat16
EPS = 1e-6
HEAD_DIM = 64
CONV_WIDTH = 31
CONV_HALO = 32
SUBLANES = 8
LANES = 128
NEG_BIG = -1e30
V7X_VMEM_LIMIT_BYTES = 56 * 2**20

A_TILE = 128
CONV_ROWS = 128
KV_TILE = 256
KV_STEP = 512
Q_TILE = 256
LOG2E = 1.4426950408889634
N_SPLIT = 3
BIAS_GROUP = 8
HIST_ROWS = 5
ONES_ROWS = 2 * SUBLANES


def _rms_rows(x, g):
    return x * lax.rsqrt(jnp.mean(x * x, axis=-1, keepdims=True) + EPS) * g


def _silu(x):
    return x * jax.nn.sigmoid(x)


def _log_sigmoid(x):
    return -(jnp.maximum(-x, 0.0) + jnp.log1p(jnp.exp(-jnp.abs(x))))


def _layernorm_rows(x, g, b):
    xc = x - jnp.mean(x, axis=-1, keepdims=True)
    return xc * lax.rsqrt(jnp.mean(xc * xc, axis=-1, keepdims=True) + EPS) * g + b


def _head_rms_t(xt, g_col, n_heads):
    n = xt.shape[-1]
    x3 = xt.reshape(n_heads, HEAD_DIM, n)
    ms = jnp.mean(x3 * x3, axis=1, keepdims=True)
    return (x3 * lax.rsqrt(ms + EPS) * g_col[None]).reshape(n_heads * HEAD_DIM, n)


def _const_spec(shape):
    nd = len(shape)
    return pl.BlockSpec(shape, lambda *_: (0,) * nd, pipeline_mode=pl.Buffered(1))


def _layer_spec(layer, shape):
    nd = len(shape)
    return pl.BlockSpec((1,) + shape, lambda *_: (layer,) + (0,) * nd, pipeline_mode=pl.Buffered(1))


def _params(*semantics):
    return pltpu.CompilerParams(dimension_semantics=semantics, vmem_limit_bytes=V7X_VMEM_LIMIT_BYTES)


def _conv_chunk(hist_ref, cw_ref, r0, rows, lsl):
    acc = None
    for phase in range(SUBLANES):
        offs = [o for o in range(2, CONV_HALO + 1) if o % SUBLANES == phase]
        n = rows if phase == 0 else rows + SUBLANES
        part = None
        for o in offs:
            term = cw_ref[o - 2:o - 1, lsl] * hist_ref[pl.ds(r0 + (o - phase), n), lsl]
            part = term if part is None else part + term
        part = part if phase == 0 else part[phase:phase + rows]
        acc = part if acc is None else acc + part
    return acc


def _a_tile_project(x, first, g_ref, win_ref, hist_ref, sz_ref):
    tt, d = x.shape

    @pl.when(first)
    def _():
        hist_ref[0:CONV_HALO, :] = jnp.zeros((CONV_HALO, d), F32)

    xn = _rms_rows(x, g_ref[...]).astype(BF16)
    u = jnp.dot(xn, win_ref[0], preferred_element_type=F32)
    hist_ref[CONV_HALO:CONV_HALO + tt, :] = u[:, :d] * jax.nn.sigmoid(u[:, d:2 * d])
    sz_ref[...] = _silu(u[:, 2 * d:])


def _a_tile_conv_piece(piece, hist_ref, cw_ref, yc_ref):
    n_lane_blocks = yc_ref.shape[1] // LANES
    c, l = divmod(piece, n_lane_blocks)
    lsl = slice(l * LANES, (l + 1) * LANES)
    yc_ref[c * CONV_ROWS:(c + 1) * CONV_ROWS, lsl] = _conv_chunk(hist_ref, cw_ref, c * CONV_ROWS, CONV_ROWS, lsl)


def _a_tile_finish(x, cb_ref, lg_ref, lb_ref, wout_ref, sz_ref, yc_ref):
    y = _silu(_layernorm_rows(yc_ref[...] + cb_ref[...], lg_ref[...], lb_ref[...]))
    m = (y * sz_ref[...]).astype(BF16)
    return x + jnp.dot(m, wout_ref[0], preferred_element_type=F32)


def _split_bf16(x):
    parts = []
    for _ in range(N_SPLIT - 1):
        hi = x.astype(BF16).astype(F32)
        parts.append(hi)
        x = x - hi
    return parts + [x]


def _kv_prompt_body(x_ref, g_ref, wk_ref, wv_ref, wft_ref, fb_ref, kg_ref, triu_ref,
                    kt_ref, vt_ref, lft_ref, kb_ref, vtb_ref, ct_ref, carry_ref, *, n_heads):
    t = pl.program_id(1)

    @pl.when(t == 0)
    def _():
        carry_ref[...] = jnp.zeros_like(carry_ref)

    xn = _rms_rows(x_ref[0], g_ref[...]).astype(BF16)
    tt = xn.shape[0]
    k = jnp.dot(xn, wk_ref[...], preferred_element_type=F32)
    v = jnp.dot(xn, wv_ref[...], preferred_element_type=F32)
    knt = _head_rms_t(k.T, kg_ref[...], n_heads)
    vt = v.T
    kt_ref[0] = knt
    vt_ref[0] = vt
    for c in range(tt // KV_TILE):
        vtb_ref[0, c] = vt[:, c * KV_TILE:(c + 1) * KV_TILE].astype(BF16)

    uft = lax.dot_general(wft_ref[...], xn, (((1,), (1,)), ((), ())), preferred_element_type=F32)
    lft = _log_sigmoid(uft + fb_ref[...])
    lft_ref[0] = lft
    ct = jnp.dot(lft, triu_ref[...], precision=lax.Precision.HIGHEST,
                 preferred_element_type=F32) + carry_ref[...]
    ct_ref[0] = ct
    carry_ref[...] = ct[:, tt - 1:tt]

    terms = _split_bf16(ct * (-LOG2E))
    row = lax.broadcasted_iota(jnp.int32, (SUBLANES, tt), 0)
    pad = jnp.zeros((HEAD_DIM - SUBLANES, tt), F32)
    blocks = []
    for h in range(n_heads):
        aug = jnp.zeros((SUBLANES, tt), F32)
        for i, term in enumerate(terms):
            aug = jnp.where(row == i, term[h:h + 1, :], aug)
        blocks += [knt[h * HEAD_DIM:(h + 1) * HEAD_DIM], aug, pad]
    kb_ref[0] = jnp.concatenate(blocks, axis=0).T.astype(BF16)


def _kv_prompt(x, g, wk, wv, wft, fb_col, kg_col):
    b, t, d = x.shape
    n_heads = wft.shape[0]
    tt = KV_STEP
    triu = jnp.triu(jnp.ones((tt, tt), F32))
    feat_major = lambda rows: pl.BlockSpec((1, rows, tt), lambda i, j: (i, 0, j))
    return pl.pallas_call(
        functools.partial(_kv_prompt_body, n_heads=n_heads),
        grid=(b, t // tt),
        in_specs=[
            pl.BlockSpec((1, tt, d), lambda i, j: (i, j, 0)),
            _const_spec((1, d)), _const_spec((d, d)), _const_spec((d, d)), _const_spec((n_heads, d)),
            _const_spec((n_heads, 1)), _const_spec((HEAD_DIM, 1)), _const_spec((tt, tt)),
        ],
        out_specs=[
            feat_major(d), feat_major(d), feat_major(n_heads),
            pl.BlockSpec((1, tt, 2 * d), lambda i, j: (i, j, 0)),
            pl.BlockSpec((1, tt // KV_TILE, d, KV_TILE), lambda i, j: (i, j, 0, 0)),
            feat_major(n_heads),
        ],
        out_shape=[
            jax.ShapeDtypeStruct((b, d, t), F32),
            jax.ShapeDtypeStruct((b, d, t), F32),
            jax.ShapeDtypeStruct((b, n_heads, t), F32),
            jax.ShapeDtypeStruct((b, t, 2 * d), BF16),
            jax.ShapeDtypeStruct((b, t // KV_TILE, d, KV_TILE), BF16),
            jax.ShapeDtypeStruct((b, n_heads, t), F32),
        ],
        scratch_shapes=[pltpu.VMEM((n_heads, 1), F32)],
        compiler_params=_params("arbitrary", "arbitrary"),
        name="kv_prompt",
    )(x, g, wk, wv, wft, fb_col, kg_col, triu)


def _b_prompt_body(x_ref, g_ref, win_ref, qg_ref, wout_ref, kb_ref, vtb_ref, ctq_ref,
                   o_ref, qm_ref, ot_ref, sz_ref, m_ref, acc_ref, *, n_heads, tq, tk):
    d = x_ref.shape[-1]
    i = pl.program_id(1)
    x = x_ref[0]
    xn = _rms_rows(x, g_ref[...]).astype(BF16)
    u = jnp.dot(xn, win_ref[0], preferred_element_type=F32)
    sz_ref[...] = _silu(u[:, d:])
    scale = LOG2E / math.sqrt(HEAD_DIM)
    qnt = (_head_rms_t(u[:, :d].T, qg_ref[...], n_heads) * scale).astype(BF16)
    ones = (lax.broadcasted_iota(jnp.int32, (HEAD_DIM, tq), 0) < N_SPLIT).astype(BF16)
    for h in range(n_heads):
        qm_ref[h, 0:HEAD_DIM, :] = qnt[h * HEAD_DIM:(h + 1) * HEAD_DIM]
        qm_ref[h, HEAD_DIM:2 * HEAD_DIM, :] = ones
        m_ref[h] = jnp.full((1, tq), NEG_BIG, F32)
        acc_ref[h] = jnp.zeros((HEAD_DIM + ONES_ROWS, tq), F32)

    causal = (lax.broadcasted_iota(jnp.int32, (tk, tq), 0) <= lax.broadcasted_iota(jnp.int32, (tk, tq), 1))
    ones_rows = jnp.ones((ONES_ROWS, tk), BF16)

    def step(j, carry, diagonal):
        k0 = pl.multiple_of(j * tk, tk)
        sts = [jnp.dot(kb_ref[0, pl.ds(k0, tk), h * 2 * HEAD_DIM:(h + 1) * 2 * HEAD_DIM], qm_ref[h],
                       preferred_element_type=F32) for h in range(n_heads)]
        ps, alphas = [], []
        for h, st in enumerate(sts):
            cq = ctq_ref[0, h:h + 1, :] * LOG2E
            if diagonal:
                st = jnp.where(causal, st, NEG_BIG)
            m = m_ref[h]
            m_new = jnp.maximum(m, jnp.max(st, axis=0, keepdims=True) + cq)
            alphas.append(jnp.exp2(m - m_new))
            ps.append(jnp.exp2(st - (m_new - cq)).astype(BF16))
            m_ref[h] = m_new
        for h, (p, alpha) in enumerate(zip(ps, alphas)):
            vh = jnp.concatenate([vtb_ref[0, j, h * HEAD_DIM:(h + 1) * HEAD_DIM, :], ones_rows], axis=0)
            acc_ref[h] = alpha * acc_ref[h] + jnp.dot(vh, p, preferred_element_type=F32)
        return carry

    lax.fori_loop(0, i, functools.partial(step, diagonal=False), 0)
    step(i, 0, True)
    for h in range(n_heads):
        ot_ref[h * HEAD_DIM:(h + 1) * HEAD_DIM, :] = acc_ref[h, 0:HEAD_DIM, :] / acc_ref[h, HEAD_DIM:HEAD_DIM + 1, :]

    m = (ot_ref[...].T * sz_ref[...]).astype(BF16)
    o_ref[0] = x + jnp.dot(m, wout_ref[0], preferred_element_type=F32)


def _b_prompt(layer, x, g, w_in, qg_col, w_out, kb, vtb, ct):
    b, t, d = x.shape
    n_heads = ct.shape[1]
    tq, tk = Q_TILE, KV_TILE
    assert tq == tk and vtb.shape == (b, t // tk, d, tk)
    return pl.pallas_call(
        functools.partial(_b_prompt_body, n_heads=n_heads, tq=tq, tk=tk),
        grid=(b, t // tq),
        in_specs=[
            pl.BlockSpec((1, tq, d), lambda i, j: (i, j, 0)),
            _const_spec((1, d)), _layer_spec(layer, (d, 2 * d)), _const_spec((HEAD_DIM, 1)),
            _layer_spec(layer, (d, d)),
            pl.BlockSpec((1, t, 2 * d), lambda i, j: (i, 0, 0)),
            pl.BlockSpec((1, t // tk, d, tk), lambda i, j: (i, 0, 0, 0)),
            pl.BlockSpec((1, n_heads, tq), lambda i, j: (i, 0, j)),
        ],
        out_specs=pl.BlockSpec((1, tq, d), lambda i, j: (i, j, 0)),
        out_shape=jax.ShapeDtypeStruct((b, t, d), F32),
        scratch_shapes=[
            pltpu.VMEM((n_heads, 2 * HEAD_DIM, tq), BF16),
            pltpu.VMEM((d, tq), F32),
            pltpu.VMEM((tq, d), F32),
            pltpu.VMEM((n_heads, 1, tq), F32),
            pltpu.VMEM((n_heads, HEAD_DIM + ONES_ROWS, tq), F32),
        ],
        compiler_params=_params("arbitrary", "arbitrary"),
        name="b_prompt",
    )(x, g, w_in, qg_col, w_out, kb, vtb, ct)


def _a_sample_body(x_ref, st_ref, cw_ref, g_ref, win_ref, cb_ref, lg_ref, lb_ref, wout_ref,
                   kvg_ref, wk_ref, wv_ref, wft_ref, fb_ref, kg_ref,
                   nst_ref, xo_ref, kst_ref, vst_ref, lfst_ref,
                   x_sc, glu_sc, sz_sc, acc_sc, carry_sc, *, n_heads):
    d = x_ref.shape[-1]
    layer = pl.program_id(0)
    s = pl.program_id(1)
    last = pl.num_programs(1) - 1
    rows = st_ref.shape[1]

    @pl.when((layer == 0) & (s == 0))
    def _():
        x_sc[...] = x_ref[...]

    @pl.when(s == 0)
    def _():
        xn = _rms_rows(x_sc[...], g_ref[0]).astype(BF16)
        u = jnp.dot(xn, win_ref[0], preferred_element_type=F32)
        glu_sc[...] = u[:, :d] * jax.nn.sigmoid(u[:, d:2 * d])
        sz_sc[...] = _silu(u[:, 2 * d:])
        acc_sc[...] = jnp.zeros_like(acc_sc)

    @pl.when(s >= 1)
    def _():
        for r in range(rows - 1):
            nst_ref[0, r] = carry_sc[r]

    @pl.when((s >= 1) & (s < last))
    def _():
        nst_ref[0, rows - 1] = st_ref[0, 0]

    @pl.when(s < last)
    def _():
        acc = acc_sc[...]
        for r in range(rows):
            acc = acc + cw_ref[0, pl.ds(rows * s + r, 1), :] * st_ref[0, r]
        acc_sc[...] = acc
        for r in range(rows - 1):
            carry_sc[r] = st_ref[0, r + 1]

    @pl.when(s == last)
    def _():
        glu = glu_sc[...]
        nst_ref[0, rows - 1] = glu
        yc = acc_sc[...] + cw_ref[0, CONV_WIDTH - 1:CONV_WIDTH, :] * glu + cb_ref[0]
        y = _silu(_layernorm_rows(yc, lg_ref[0], lb_ref[0]))
        m = (y * sz_sc[...]).astype(BF16)
        x_sc[...] = x_sc[...] + jnp.dot(m, wout_ref[0], preferred_element_type=F32)

    @pl.when((s == last) & (layer == pl.num_programs(0) - 1))
    def _():
        x = x_sc[...]
        xo_ref[...] = x
        xn = _rms_rows(x, kvg_ref[...]).astype(BF16)
        k = jnp.dot(xn, wk_ref[...], preferred_element_type=F32)
        v = jnp.dot(xn, wv_ref[...], preferred_element_type=F32)
        kst_ref[...] = _head_rms_t(k.T, kg_ref[...], n_heads)
        vst_ref[...] = v.T
        uft = lax.dot_general(wft_ref[...], xn, (((1,), (1,)), ((), ())), preferred_element_type=F32)
        lfst_ref[...] = _log_sigmoid(uft + fb_ref[...])


def _a_sample(x, st, cw, g, w_in, cb, lg, lb, w_out, kvg, wk, wv, wft, fb_col, kg_col):
    n, d = x.shape
    n_layers, n_hist = st.shape[0], st.shape[1]
    n_heads = wft.shape[0]
    rows = HIST_ROWS
    assert n_hist % rows == 0 and cw.shape[1] == n_hist + 1
    n_blocks = n_hist // rows
    per_layer = lambda *shape: pl.BlockSpec((1,) + shape, lambda l, s: (l,) + (0,) * len(shape))
    return pl.pallas_call(
        functools.partial(_a_sample_body, n_heads=n_heads),
        grid=(n_layers, n_blocks + 1),
        in_specs=[
            _const_spec((n, d)),
            pl.BlockSpec((1, rows, n, d), lambda l, s: (l, jnp.minimum(s, n_blocks - 1), 0, 0)),
            per_layer(n_hist + 1, d),
            per_layer(1, d), per_layer(d, 3 * d), per_layer(1, d), per_layer(1, d), per_layer(1, d),
            per_layer(d, d),
            _const_spec((1, d)), _const_spec((d, d)), _const_spec((d, d)), _const_spec((n_heads, d)),
            _const_spec((n_heads, 1)), _const_spec((HEAD_DIM, 1)),
        ],
        out_specs=[
            pl.BlockSpec((1, rows, n, d), lambda l, s: (l, jnp.maximum(s - 1, 0), 0, 0)),
            pl.BlockSpec((n, d), lambda l, s: (0, 0)),
            pl.BlockSpec((d, n), lambda l, s: (0, 0)),
            pl.BlockSpec((d, n), lambda l, s: (0, 0)),
            pl.BlockSpec((n_heads, n), lambda l, s: (0, 0)),
        ],
        out_shape=[
            jax.ShapeDtypeStruct((n_layers, n_hist, n, d), F32),
            jax.ShapeDtypeStruct((n, d), F32),
            jax.ShapeDtypeStruct((d, n), F32),
            jax.ShapeDtypeStruct((d, n), F32),
            jax.ShapeDtypeStruct((n_heads, n), F32),
        ],
        scratch_shapes=[pltpu.VMEM((n, d), F32)] * 4 + [pltpu.VMEM((rows - 1, n, d), F32)],
        compiler_params=_params("arbitrary", "arbitrary"),
        name="a_sample",
    )(x, st, cw, g, w_in, cb, lg, lb, w_out, kvg, wk, wv, wft, fb_col, kg_col)


def _bias_sample_body(pt_ref, lfc_ref, lfst_ref, lower_ref, o_ref, *, n_pages, group):
    g = pl.program_id(0)
    pages = [[lfc_ref[pt_ref[g * group + r, j]] for j in range(n_pages)] for r in range(group)]
    n_heads, page = pages[0][0].shape
    stacked = jnp.concatenate([pg for per_sample in pages for pg in per_sample], axis=0)
    within = sum(jnp.dot(term.astype(BF16), lower_ref[...], preferred_element_type=F32)
                 for term in _split_bf16(stacked))
    lane = lax.broadcasted_iota(jnp.int32, lfst_ref.shape, 1)
    for r in range(group):
        carry = jnp.sum(jnp.where(lane == g * group + r, lfst_ref[...], 0.0), axis=1, keepdims=True)
        for j in reversed(range(n_pages)):
            row0 = (r * n_pages + j) * n_heads
            o_ref[r, :, j * page:(j + 1) * page] = within[row0:row0 + n_heads] + carry
            carry = carry + jnp.sum(pages[r][j], axis=1, keepdims=True)


def _bias_sample(page_table, lfc, lfst):
    n, n_pages = page_table.shape
    n_pool, n_heads, page = lfc.shape
    group = BIAS_GROUP
    lower = jnp.tril(jnp.ones((page, page), BF16), k=-1)
    return pl.pallas_call(
        functools.partial(_bias_sample_body, n_pages=n_pages, group=group),
        grid_spec=pltpu.PrefetchScalarGridSpec(
            num_scalar_prefetch=1,
            grid=(n // group,),
            in_specs=[_const_spec((n_pool, n_heads, page)), _const_spec((n_heads, n)), _const_spec((page, page))],
            out_specs=pl.BlockSpec((group, n_heads, n_pages * page), lambda i, pt: (i, 0, 0)),
        ),
        out_shape=jax.ShapeDtypeStruct((n, n_heads, n_pages * page), F32),
        compiler_params=_params("arbitrary"),
        name="bias_sample",
    )(page_table, lfc, lfst, lower)


def _sample_scores(b, q_row, bias_ref, kst_ref, k_refs, n_heads):
    d = q_row.shape[-1]
    n = kst_ref.shape[-1]
    own = (lax.broadcasted_iota(jnp.int32, (n_heads, d), 1) // HEAD_DIM
           == lax.broadcasted_iota(jnp.int32, (n_heads, d), 0))
    qbd = jnp.where(own, jnp.broadcast_to(q_row, (n_heads, d)), 0.0).astype(BF16)
    s = jnp.concatenate(
        [jnp.dot(qbd, k_ref[0].astype(BF16), preferred_element_type=F32) for k_ref in k_refs],
        axis=1) + bias_ref[0]
    is_b = lax.broadcasted_iota(jnp.int32, (n_heads, n), 1) == b
    k_self = jnp.where(lax.broadcasted_iota(jnp.int32, (d, n), 1) == b, kst_ref[...], 0.0).astype(BF16)
    s_self = jnp.where(is_b, jnp.dot(qbd, k_self, preferred_element_type=F32), NEG_BIG)
    m = jnp.maximum(jnp.max(s, axis=1, keepdims=True), jnp.max(s_self, axis=1, keepdims=True))
    p = jnp.exp(s - m)
    p_self = jnp.exp(s_self - m)
    l = jnp.sum(p, axis=1, keepdims=True) + jnp.sum(p_self, axis=1, keepdims=True)
    return p, p_self, l


def _weighted_values(p_pages, v_pages, h):
    rows = slice(h * HEAD_DIM, (h + 1) * HEAD_DIM)
    acc = None
    for pw, v in zip(p_pages, v_pages):
        term = v[rows, :] * pw[h:h + 1, :]
        acc = term if acc is None else acc + term
    return jnp.sum(acc, axis=1, keepdims=True)


def _ab_body(pt_ref, xs_ref, gb_ref, winb_ref, qg_ref, woutb_ref, bias_ref, kst_ref, vst_ref,
             xp_ref, ga_ref, wina_ref, cw_ref, cb_ref, lg_ref, lb_ref, wouta_ref, *rest,
             n_heads, n_pages, tiles_per_seq):
    k_refs = rest[:n_pages]
    v_refs = rest[n_pages:2 * n_pages]
    ys_ref, yp_ref, st_ref, q_sc, szs_sc, ot_sc, hist_ref, sza_ref, yc_ref = rest[2 * n_pages:]
    n, d = xs_ref.shape
    tt = xp_ref.shape[1]
    i = pl.program_id(0)
    t = i % tiles_per_seq

    @pl.when(i == 0)
    def _():
        xn = _rms_rows(xs_ref[...], gb_ref[...]).astype(BF16)
        u = jnp.dot(xn, winb_ref[0], preferred_element_type=F32)
        scale = 1.0 / math.sqrt(HEAD_DIM)
        q_sc[...] = (_head_rms_t(u[:, :d].T, qg_ref[...], n_heads) * scale).T
        szs_sc[...] = _silu(u[:, d:])
        ot_sc[...] = jnp.zeros_like(ot_sc)

    x = xp_ref[0]
    _a_tile_project(x, t == 0, ga_ref, wina_ref, hist_ref, sza_ref)
    p, p_self, l = _sample_scores(i, q_sc[pl.ds(i, 1), :], bias_ref, kst_ref, k_refs, n_heads)
    for piece in range((tt // CONV_ROWS) * (d // LANES)):
        _a_tile_conv_piece(piece, hist_ref, cw_ref, yc_ref)
    page = p_self.shape[-1]
    p_pages = [p_self] + [p[:, j * page:(j + 1) * page] for j in range(n_pages)]
    v_pages = [vst_ref] + [v_ref.at[0] for v_ref in v_refs]
    inv_l = 1.0 / l
    is_i = lax.broadcasted_iota(jnp.int32, (HEAD_DIM, n), 1) == i
    for h in range(n_heads):
        rows = slice(h * HEAD_DIM, (h + 1) * HEAD_DIM)
        oh = _weighted_values(p_pages, v_pages, h) * inv_l[h:h + 1, :]
        ot_sc[rows, :] = jnp.where(is_i, oh, ot_sc[rows, :])
    yp_ref[0] = _a_tile_finish(x, cb_ref, lg_ref, lb_ref, wouta_ref, sza_ref, yc_ref)

    @pl.when(t == tiles_per_seq - 1)
    def _():
        st_ref[0] = hist_ref[tt + CONV_HALO - (CONV_WIDTH - 1):tt + CONV_HALO, :]

    hist_ref[0:CONV_HALO, :] = hist_ref[tt:tt + CONV_HALO, :]

    @pl.when(i == pl.num_programs(0) - 1)
    def _():
        mm = (ot_sc[...].T * szs_sc[...]).astype(BF16)
        ys_ref[...] = xs_ref[...] + jnp.dot(mm, woutb_ref[0], preferred_element_type=F32)


def _ab_layer(layer, page_table, xs, gb, w_in_b, qg_col, w_out_b, bias, kst, vst, ckt, cvt,
              xp, ga, w_in_a, cw, cb, lg, lb, w_out_a):
    n, d = xs.shape
    bp, tp, _ = xp.shape
    n_pages = page_table.shape[1]
    n_heads = bias.shape[1]
    page = ckt.shape[-1]
    tt = A_TILE
    tiles_per_seq = tp // tt
    assert page == n, "the new token's key rides through the kernel as one more page-sized block"
    assert bp * tiles_per_seq == n, "one prompt tile per sample grid step"
    page_spec = lambda j: pl.BlockSpec((1, d, page), lambda i, pt, j=j: (pt[i, j], 0, 0))
    tile_spec = pl.BlockSpec((1, tt, d), lambda i, pt: (i // tiles_per_seq, i % tiles_per_seq, 0))
    row = lambda: _const_spec((1, d))
    return pl.pallas_call(
        functools.partial(_ab_body, n_heads=n_heads, n_pages=n_pages, tiles_per_seq=tiles_per_seq),
        grid_spec=pltpu.PrefetchScalarGridSpec(
            num_scalar_prefetch=1,
            grid=(n,),
            in_specs=[
                _const_spec((n, d)), row(), _layer_spec(layer, (d, 2 * d)), _const_spec((HEAD_DIM, 1)),
                _layer_spec(layer, (d, d)),
                pl.BlockSpec((1, n_heads, n_pages * page), lambda i, pt: (i, 0, 0)),
                _const_spec((d, n)), _const_spec((d, n)),
                tile_spec, row(), _layer_spec(layer, (d, 3 * d)), _const_spec((CONV_WIDTH, d)), row(), row(), row(),
                _layer_spec(layer, (d, d)),
            ] + [page_spec(j) for j in range(n_pages)] * 2,
            out_specs=[
                pl.BlockSpec((n, d), lambda i, pt: (0, 0)),
                tile_spec,
                pl.BlockSpec((1, CONV_WIDTH - 1, d), lambda i, pt: (i // tiles_per_seq, 0, 0)),
            ],
            scratch_shapes=[pltpu.VMEM((n, d), F32)] * 2 + [pltpu.VMEM((d, n), F32)] + [
                pltpu.VMEM((tt + CONV_HALO, d), F32), pltpu.VMEM((tt, d), F32), pltpu.VMEM((tt, d), F32)],
        ),
        out_shape=[
            jax.ShapeDtypeStruct((n, d), F32),
            jax.ShapeDtypeStruct((bp, tp, d), F32),
            jax.ShapeDtypeStruct((bp, CONV_WIDTH - 1, d), F32),
        ],
        compiler_params=_params("arbitrary"),
        name="ab_layer",
    )(page_table, xs, gb, w_in_b, qg_col, w_out_b, bias, kst, vst,
      xp, ga, w_in_a, cw, cb, lg, lb, w_out_a, *([ckt] * n_pages), *([cvt] * n_pages))


def kernel(x_prompt, x_sample, state_conv, cache_k, cache_v, cache_logf, page_table, a_norm, a_w_in, a_conv_w,
           a_conv_b, a_ln_g, a_ln_b, a_w_out, kv_norm, kv_w, kv_fb, k_norm, b_norm, b_w_in, q_norm, b_w_out):
    bp, tp, d = x_prompt.shape
    bs = x_sample.shape[0]
    n_a = a_w_in.shape[0]
    n_b = b_w_in.shape[0]
    n_heads = kv_fb.shape[0]
    attn = n_heads * HEAD_DIM
    n_pool, page = cache_k.shape[0], cache_k.shape[1]

    a_w_in_b = a_w_in.astype(BF16)
    a_w_out_b = a_w_out.astype(BF16)
    b_w_in_b = b_w_in.astype(BF16)
    b_w_out_b = b_w_out.astype(BF16)
    wk = kv_w[:, :attn].astype(BF16)
    wv = kv_w[:, attn:2 * attn].astype(BF16)
    wft = kv_w[:, 2 * attn:].T.astype(BF16)
    fb_col = kv_fb.reshape(n_heads, 1)
    kg_col = k_norm.reshape(HEAD_DIM, 1)
    kvg = kv_norm.reshape(1, d)

    st_t = jnp.transpose(state_conv, (0, 2, 1, 3))
    nst_t, xs, kst, vst, lfst = _a_sample(
        x_sample.reshape(bs, d), st_t, a_conv_w, a_norm[:, None], a_w_in_b, a_conv_b[:, None],
        a_ln_g[:, None], a_ln_b[:, None], a_w_out_b, kvg, wk, wv, wft, fb_col, kg_col)
    ckt = jnp.transpose(cache_k, (0, 2, 3, 1)).reshape(n_pool, attn, page)
    cvt = jnp.transpose(cache_v, (0, 2, 3, 1)).reshape(n_pool, attn, page)
    lfc = jnp.transpose(cache_logf, (0, 2, 1))
    bias = _bias_sample(page_table, lfc, lfst)

    assert n_a == n_b
    xp = x_prompt
    conv_p = []
    for j in range(n_b):
        xs, xp, st = _ab_layer(
            j, page_table, xs, b_norm[j][None], b_w_in_b, q_norm[j].reshape(HEAD_DIM, 1), b_w_out_b,
            bias, kst, vst, ckt, cvt,
            xp, a_norm[j][None], a_w_in_b, a_conv_w[j], a_conv_b[j][None], a_ln_g[j][None], a_ln_b[j][None],
            a_w_out_b)
        conv_p.append(st)

    kt, vt, lft, kb, vtb, ct = _kv_prompt(xp, kvg, wk, wv, wft, fb_col, kg_col)
    for j in range(n_b):
        xp = _b_prompt(j, xp, b_norm[j][None], b_w_in_b, q_norm[j].reshape(HEAD_DIM, 1), b_w_out_b,
                       kb, vtb, ct)

    heads_last = lambda a: jnp.transpose(a.reshape(a.shape[0], n_heads, HEAD_DIM, a.shape[-1]), (0, 3, 1, 2))
    return (
        xp,
        xs.reshape(bs, 1, d),
        jnp.stack(conv_p, axis=0),
        jnp.transpose(nst_t, (0, 2, 1, 3)),
        heads_last(kt),
        heads_last(vt),
        jnp.transpose(lft, (0, 2, 1)),
        jnp.transpose(kst.reshape(n_heads, HEAD_DIM, bs), (2, 0, 1))[:, None],
        jnp.transpose(vst.reshape(n_heads, HEAD_DIM, bs), (2, 0, 1))[:, None],
        jnp.transpose(lfst, (1, 0))[:, None],
    )
```

```python
import functools
import math

import jax
import jax.numpy as jnp
from jax import lax
from jax.experimental import pallas as pl
from jax.experimental.pallas import tpu as pltpu

F32 = jnp.float32
BF16 = jnp.bfloat16
EPS = 1e-6
HEAD_DIM = 64
CONV_WIDTH = 31
CONV_HALO = 32
SUBLANES = 8
LANES = 128
NEG_BIG = -1e30
V7X_VMEM_LIMIT_BYTES = 56 * 2**20

A_TILE = 128
CONV_ROWS = 128
KV_TILE = 256
KV_STEP = 512
Q_TILE = 256
LOG2E = 1.4426950408889634
N_SPLIT = 3
BIAS_GROUP = 8
HIST_ROWS = 5
ONES_ROWS = 2 * SUBLANES


def _rms_rows(x, g):
    return x * lax.rsqrt(jnp.mean(x * x, axis=-1, keepdims=True) + EPS) * g


def _silu(x):
    return x * jax.nn.sigmoid(x)


def _log_sigmoid(x):
    return -(jnp.maximum(-x, 0.0) + jnp.log1p(jnp.exp(-jnp.abs(x))))


def _layernorm_rows(x, g, b):
    xc = x - jnp.mean(x, axis=-1, keepdims=True)
    return xc * lax.rsqrt(jnp.mean(xc * xc, axis=-1, keepdims=True) + EPS) * g + b


def _head_rms_t(xt, g_col, n_heads):
    n = xt.shape[-1]
    x3 = xt.reshape(n_heads, HEAD_DIM, n)
    ms = jnp.mean(x3 * x3, axis=1, keepdims=True)
    return (x3 * lax.rsqrt(ms + EPS) * g_col[None]).reshape(n_heads * HEAD_DIM, n)


def _const_spec(shape):
    nd = len(shape)
    return pl.BlockSpec(shape, lambda *_: (0,) * nd, pipeline_mode=pl.Buffered(1))


def _layer_spec(layer, shape):
    nd = len(shape)
    return pl.BlockSpec((1,) + shape, lambda *_: (layer,) + (0,) * nd, pipeline_mode=pl.Buffered(1))


def _params(*semantics):
    return pltpu.CompilerParams(dimension_semantics=semantics, vmem_limit_bytes=V7X_VMEM_LIMIT_BYTES)


def _conv_chunk(hist_ref, cw_ref, r0, rows, lsl):
    acc = None
    for phase in range(SUBLANES):
        offs = [o for o in range(2, CONV_HALO + 1) if o % SUBLANES == phase]
        n = rows if phase == 0 else rows + SUBLANES
        part = None
        for o in offs:
            term = cw_ref[o - 2:o - 1, lsl] * hist_ref[pl.ds(r0 + (o - phase), n), lsl]
            part = term if part is None else part + term
        if phase:
            tiles = pltpu.roll(part.reshape(n // SUBLANES, SUBLANES, part.shape[-1]), SUBLANES - phase, axis=1)
            sub = lax.broadcasted_iota(jnp.int32, tiles[:-1].shape, 1)
            part = jnp.where(sub < SUBLANES - phase, tiles[:-1], tiles[1:]).reshape(rows, part.shape[-1])
        acc = part if acc is None else acc + part
    return acc


def _a_tile_project(x, first, g_ref, win_ref, hist_ref, sz_ref):
    tt, d = x.shape

    @pl.when(first)
    def _():
        hist_ref[0:CONV_HALO, :] = jnp.zeros((CONV_HALO, d), F32)

    xn = _rms_rows(x, g_ref[...]).astype(BF16)
    u = jnp.dot(xn, win_ref[0], preferred_element_type=F32)
    hist_ref[CONV_HALO:CONV_HALO + tt, :] = u[:, :d] * jax.nn.sigmoid(u[:, d:2 * d])
    sz_ref[...] = _silu(u[:, 2 * d:])


def _a_tile_conv_piece(piece, hist_ref, cw_ref, yc_ref):
    n_lane_blocks = yc_ref.shape[1] // LANES
    c, l = divmod(piece, n_lane_blocks)
    lsl = slice(l * LANES, (l + 1) * LANES)
    yc_ref[c * CONV_ROWS:(c + 1) * CONV_ROWS, lsl] = _conv_chunk(hist_ref, cw_ref, c * CONV_ROWS, CONV_ROWS, lsl)


def _a_tile_finish(x, cb_ref, lg_ref, lb_ref, wout_ref, sz_ref, yc_ref):
    y = _silu(_layernorm_rows(yc_ref[...] + cb_ref[...], lg_ref[...], lb_ref[...]))
    m = (y * sz_ref[...]).astype(BF16)
    return x + jnp.dot(m, wout_ref[0], preferred_element_type=F32)


def _split_bf16(x):
    parts = []
    for _ in range(N_SPLIT - 1):
        hi = x.astype(BF16).astype(F32)
        parts.append(hi)
        x = x - hi
    return parts + [x]


def _kv_prompt_body(x_ref, g_ref, wk_ref, wv_ref, wft_ref, fb_ref, kg_ref, triu_ref,
                    kt_ref, vt_ref, lft_ref, kb_ref, vtb_ref, ct_ref, carry_ref, *, n_heads):
    t = pl.program_id(1)

    @pl.when(t == 0)
    def _():
        carry_ref[...] = jnp.zeros_like(carry_ref)

    xn = _rms_rows(x_ref[0], g_ref[...]).astype(BF16)
    tt = xn.shape[0]
    k = jnp.dot(xn, wk_ref[...], preferred_element_type=F32)
    v = jnp.dot(xn, wv_ref[...], preferred_element_type=F32)
    knt = _head_rms_t(k.T, kg_ref[...], n_heads)
    vt = v.T
    kt_ref[0] = knt
    vt_ref[0] = vt
    for c in range(tt // KV_TILE):
        vtb_ref[0, c] = vt[:, c * KV_TILE:(c + 1) * KV_TILE].astype(BF16)

    uft = lax.dot_general(wft_ref[...], xn, (((1,), (1,)), ((), ())), preferred_element_type=F32)
    lft = _log_sigmoid(uft + fb_ref[...])
    lft_ref[0] = lft
    ct = jnp.dot(lft, triu_ref[...], precision=lax.Precision.HIGHEST,
                 preferred_element_type=F32) + carry_ref[...]
    ct_ref[0] = ct
    carry_ref[...] = ct[:, tt - 1:tt]

    terms = _split_bf16(ct * (-LOG2E))
    row = lax.broadcasted_iota(jnp.int32, (SUBLANES, tt), 0)
    pad = jnp.zeros((HEAD_DIM - SUBLANES, tt), F32)
    blocks = []
    for h in range(n_heads):
        aug = jnp.zeros((SUBLANES, tt), F32)
        for i, term in enumerate(terms):
            aug = jnp.where(row == i, term[h:h + 1, :], aug)
        blocks += [knt[h * HEAD_DIM:(h + 1) * HEAD_DIM], aug, pad]
    kb_ref[0] = jnp.concatenate(blocks, axis=0).T.astype(BF16)


def _kv_prompt(x, g, wk, wv, wft, fb_col, kg_col):
    b, t, d = x.shape
    n_heads = wft.shape[0]
    tt = KV_STEP
    triu = jnp.triu(jnp.ones((tt, tt), F32))
    feat_major = lambda rows: pl.BlockSpec((1, rows, tt), lambda i, j: (i, 0, j))
    return pl.pallas_call(
        functools.partial(_kv_prompt_body, n_heads=n_heads),
        grid=(b, t // tt),
        in_specs=[
            pl.BlockSpec((1, tt, d), lambda i, j: (i, j, 0)),
            _const_spec((1, d)), _const_spec((d, d)), _const_spec((d, d)), _const_spec((n_heads, d)),
            _const_spec((n_heads, 1)), _const_spec((HEAD_DIM, 1)), _const_spec((tt, tt)),
        ],
        out_specs=[
            feat_major(d), feat_major(d), feat_major(n_heads),
            pl.BlockSpec((1, tt, 2 * d), lambda i, j: (i, j, 0)),
            pl.BlockSpec((1, tt // KV_TILE, d, KV_TILE), lambda i, j: (i, j, 0, 0)),
            feat_major(n_heads),
        ],
        out_shape=[
            jax.ShapeDtypeStruct((b, d, t), F32),
            jax.ShapeDtypeStruct((b, d, t), F32),
            jax.ShapeDtypeStruct((b, n_heads, t), F32),
            jax.ShapeDtypeStruct((b, t, 2 * d), BF16),
            jax.ShapeDtypeStruct((b, t // KV_TILE, d, KV_TILE), BF16),
            jax.ShapeDtypeStruct((b, n_heads, t), F32),
        ],
        scratch_shapes=[pltpu.VMEM((n_heads, 1), F32)],
        compiler_params=_params("arbitrary", "arbitrary"),
        name="kv_prompt",
    )(x, g, wk, wv, wft, fb_col, kg_col, triu)


def _b_prompt_body(x_ref, g_ref, win_ref, qg_ref, wout_ref, kb_ref, vtb_ref, ctq_ref,
                   o_ref, qm_ref, ot_ref, sz_ref, m_ref, acc_ref, *, n_heads, tq, tk):
    d = x_ref.shape[-1]
    i = pl.program_id(1)
    x = x_ref[0]
    xn = _rms_rows(x, g_ref[...]).astype(BF16)
    u = jnp.dot(xn, win_ref[0], preferred_element_type=F32)
    sz_ref[...] = _silu(u[:, d:])
    scale = LOG2E / math.sqrt(HEAD_DIM)
    qnt = (_head_rms_t(u[:, :d].T, qg_ref[...], n_heads) * scale).astype(BF16)
    ones = (lax.broadcasted_iota(jnp.int32, (HEAD_DIM, tq), 0) < N_SPLIT).astype(BF16)
    for h in range(n_heads):
        qm_ref[h, 0:HEAD_DIM, :] = qnt[h * HEAD_DIM:(h + 1) * HEAD_DIM]
        qm_ref[h, HEAD_DIM:2 * HEAD_DIM, :] = ones
        m_ref[h] = jnp.full((1, tq), NEG_BIG, F32)
        acc_ref[h] = jnp.zeros((HEAD_DIM + ONES_ROWS, tq), F32)

    causal = (lax.broadcasted_iota(jnp.int32, (tk, tq), 0) <= lax.broadcasted_iota(jnp.int32, (tk, tq), 1))
    ones_rows = jnp.ones((ONES_ROWS, tk), BF16)

    def step(j, carry, diagonal):
        k0 = pl.multiple_of(j * tk, tk)
        sts = [jnp.dot(kb_ref[0, pl.ds(k0, tk), h * 2 * HEAD_DIM:(h + 1) * 2 * HEAD_DIM], qm_ref[h],
                       preferred_element_type=F32) for h in range(n_heads)]
        ps, alphas = [], []
        for h, st in enumerate(sts):
            cq = ctq_ref[0, h:h + 1, :] * LOG2E
            if diagonal:
                st = jnp.where(causal, st, NEG_BIG)
            m = m_ref[h]
            m_new = jnp.maximum(m, jnp.max(st, axis=0, keepdims=True) + cq)
            alphas.append(jnp.exp2(m - m_new))
            ps.append(jnp.exp2(st - (m_new - cq)).astype(BF16))
            m_ref[h] = m_new
        for h, (p, alpha) in enumerate(zip(ps, alphas)):
            vh = jnp.concatenate([vtb_ref[0, j, h * HEAD_DIM:(h + 1) * HEAD_DIM, :], ones_rows], axis=0)
            acc_ref[h] = alpha * acc_ref[h] + jnp.dot(vh, p, preferred_element_type=F32)
        return carry

    lax.fori_loop(0, i, functools.partial(step, diagonal=False), 0)
    step(i, 0, True)
    for h in range(n_heads):
        ot_ref[h * HEAD_DIM:(h + 1) * HEAD_DIM, :] = acc_ref[h, 0:HEAD_DIM, :] / acc_ref[h, HEAD_DIM:HEAD_DIM + 1, :]

    m = (ot_ref[...].T * sz_ref[...]).astype(BF16)
    o_ref[0] = x + jnp.dot(m, wout_ref[0], preferred_element_type=F32)


def _b_prompt(layer, x, g, w_in, qg_col, w_out, kb, vtb, ct):
    b, t, d = x.shape
    n_heads = ct.shape[1]
    tq, tk = Q_TILE, KV_TILE
    assert tq == tk and vtb.shape == (b, t // tk, d, tk)
    return pl.pallas_call(
        functools.partial(_b_prompt_body, n_heads=n_heads, tq=tq, tk=tk),
        grid=(b, t // tq),
        in_specs=[
            pl.BlockSpec((1, tq, d), lambda i, j: (i, j, 0)),
            _const_spec((1, d)), _layer_spec(layer, (d, 2 * d)), _const_spec((HEAD_DIM, 1)),
            _layer_spec(layer, (d, d)),
            pl.BlockSpec((1, t, 2 * d), lambda i, j: (i, 0, 0)),
            pl.BlockSpec((1, t // tk, d, tk), lambda i, j: (i, 0, 0, 0)),
            pl.BlockSpec((1, n_heads, tq), lambda i, j: (i, 0, j)),
        ],
        out_specs=pl.BlockSpec((1, tq, d), lambda i, j: (i, j, 0)),
        out_shape=jax.ShapeDtypeStruct((b, t, d), F32),
        scratch_shapes=[
            pltpu.VMEM((n_heads, 2 * HEAD_DIM, tq), BF16),
            pltpu.VMEM((d, tq), F32),
            pltpu.VMEM((tq, d), F32),
            pltpu.VMEM((n_heads, 1, tq), F32),
            pltpu.VMEM((n_heads, HEAD_DIM + ONES_ROWS, tq), F32),
        ],
        compiler_params=_params("arbitrary", "arbitrary"),
        name="b_prompt",
    )(x, g, w_in, qg_col, w_out, kb, vtb, ct)


def _a_sample_body(x_ref, st_ref, cw_ref, g_ref, win_ref, cb_ref, lg_ref, lb_ref, wout_ref,
                   kvg_ref, wk_ref, wv_ref, wft_ref, fb_ref, kg_ref,
                   nst_ref, xo_ref, kst_ref, vst_ref, lfst_ref,
                   x_sc, glu_sc, sz_sc, acc_sc, carry_sc, *, n_heads):
    d = x_ref.shape[-1]
    layer = pl.program_id(0)
    s = pl.program_id(1)
    last = pl.num_programs(1) - 1
    rows = st_ref.shape[1]

    @pl.when((layer == 0) & (s == 0))
    def _():
        x_sc[...] = x_ref[...]

    @pl.when(s == 0)
    def _():
        xn = _rms_rows(x_sc[...], g_ref[0]).astype(BF16)
        u = jnp.dot(xn, win_ref[0], preferred_element_type=F32)
        glu_sc[...] = u[:, :d] * jax.nn.sigmoid(u[:, d:2 * d])
        sz_sc[...] = _silu(u[:, 2 * d:])
        acc_sc[...] = jnp.zeros_like(acc_sc)

    @pl.when(s >= 1)
    def _():
        for r in range(rows - 1):
            nst_ref[0, r] = carry_sc[r]

    @pl.when((s >= 1) & (s < last))
    def _():
        nst_ref[0, rows - 1] = st_ref[0, 0]

    @pl.when(s < last)
    def _():
        acc = acc_sc[...]
        for r in range(rows):
            acc = acc + cw_ref[0, pl.ds(rows * s + r, 1), :] * st_ref[0, r]
        acc_sc[...] = acc
        for r in range(rows - 1):
            carry_sc[r] = st_ref[0, r + 1]

    @pl.when(s == last)
    def _():
        glu = glu_sc[...]
        nst_ref[0, rows - 1] = glu
        yc = acc_sc[...] + cw_ref[0, CONV_WIDTH - 1:CONV_WIDTH, :] * glu + cb_ref[0]
        y = _silu(_layernorm_rows(yc, lg_ref[0], lb_ref[0]))
        m = (y * sz_sc[...]).astype(BF16)
        x_sc[...] = x_sc[...] + jnp.dot(m, wout_ref[0], preferred_element_type=F32)

    @pl.when((s == last) & (layer == pl.num_programs(0) - 1))
    def _():
        x = x_sc[...]
        xo_ref[...] = x
        xn = _rms_rows(x, kvg_ref[...]).astype(BF16)
        k = jnp.dot(xn, wk_ref[...], preferred_element_type=F32)
        v = jnp.dot(xn, wv_ref[...], preferred_element_type=F32)
        kst_ref[...] = _head_rms_t(k.T, kg_ref[...], n_heads)
        vst_ref[...] = v.T
        uft = lax.dot_general(wft_ref[...], xn, (((1,), (1,)), ((), ())), preferred_element_type=F32)
        lfst_ref[...] = _log_sigmoid(uft + fb_ref[...])


def _a_sample(x, st, cw, g, w_in, cb, lg, lb, w_out, kvg, wk, wv, wft, fb_col, kg_col):
    n, d = x.shape
    n_layers, n_hist = st.shape[0], st.shape[1]
    n_heads = wft.shape[0]
    rows = HIST_ROWS
    assert n_hist % rows == 0 and cw.shape[1] == n_hist + 1
    n_blocks = n_hist // rows
    per_layer = lambda *shape: pl.BlockSpec((1,) + shape, lambda l, s: (l,) + (0,) * len(shape))
    return pl.pallas_call(
        functools.partial(_a_sample_body, n_heads=n_heads),
        grid=(n_layers, n_blocks + 1),
        in_specs=[
            _const_spec((n, d)),
            pl.BlockSpec((1, rows, n, d), lambda l, s: (l, jnp.minimum(s, n_blocks - 1), 0, 0)),
            per_layer(n_hist + 1, d),
            per_layer(1, d), per_layer(d, 3 * d), per_layer(1, d), per_layer(1, d), per_layer(1, d),
            per_layer(d, d),
            _const_spec((1, d)), _const_spec((d, d)), _const_spec((d, d)), _const_spec((n_heads, d)),
            _const_spec((n_heads, 1)), _const_spec((HEAD_DIM, 1)),
        ],
        out_specs=[
            pl.BlockSpec((1, rows, n, d), lambda l, s: (l, jnp.maximum(s - 1, 0), 0, 0)),
            pl.BlockSpec((n, d), lambda l, s: (0, 0)),
            pl.BlockSpec((d, n), lambda l, s: (0, 0)),
            pl.BlockSpec((d, n), lambda l, s: (0, 0)),
            pl.BlockSpec((n_heads, n), lambda l, s: (0, 0)),
        ],
        out_shape=[
            jax.ShapeDtypeStruct((n_layers, n_hist, n, d), F32),
            jax.ShapeDtypeStruct((n, d), F32),
            jax.ShapeDtypeStruct((d, n), F32),
            jax.ShapeDtypeStruct((d, n), F32),
            jax.ShapeDtypeStruct((n_heads, n), F32),
        ],
        scratch_shapes=[pltpu.VMEM((n, d), F32)] * 4 + [pltpu.VMEM((rows - 1, n, d), F32)],
        compiler_params=_params("arbitrary", "arbitrary"),
        name="a_sample",
    )(x, st, cw, g, w_in, cb, lg, lb, w_out, kvg, wk, wv, wft, fb_col, kg_col)


def _bias_sample_body(pt_ref, lfc_ref, lfst_ref, lower_ref, o_ref, *, n_pages, group):
    g = pl.program_id(0)
    pages = [[lfc_ref[pt_ref[g * group + r, j]] for j in range(n_pages)] for r in range(group)]
    n_heads, page = pages[0][0].shape
    stacked = jnp.concatenate([pg for per_sample in pages for pg in per_sample], axis=0)
    within = sum(jnp.dot(term.astype(BF16), lower_ref[...], preferred_element_type=F32)
                 for term in _split_bf16(stacked))
    lane = lax.broadcasted_iota(jnp.int32, lfst_ref.shape, 1)
    for r in range(group):
        carry = jnp.sum(jnp.where(lane == g * group + r, lfst_ref[...], 0.0), axis=1, keepdims=True)
        for j in reversed(range(n_pages)):
            row0 = (r * n_pages + j) * n_heads
            o_ref[r, :, j * page:(j + 1) * page] = within[row0:row0 + n_heads] + carry
            carry = carry + jnp.sum(pages[r][j], axis=1, keepdims=True)


def _bias_sample(page_table, lfc, lfst):
    n, n_pages = page_table.shape
    n_pool, n_heads, page = lfc.shape
    group = BIAS_GROUP
    lower = jnp.tril(jnp.ones((page, page), BF16), k=-1)
    return pl.pallas_call(
        functools.partial(_bias_sample_body, n_pages=n_pages, group=group),
        grid_spec=pltpu.PrefetchScalarGridSpec(
            num_scalar_prefetch=1,
            grid=(n // group,),
            in_specs=[_const_spec((n_pool, n_heads, page)), _const_spec((n_heads, n)), _const_spec((page, page))],
            out_specs=pl.BlockSpec((group, n_heads, n_pages * page), lambda i, pt: (i, 0, 0)),
        ),
        out_shape=jax.ShapeDtypeStruct((n, n_heads, n_pages * page), F32),
        compiler_params=_params("arbitrary"),
        name="bias_sample",
    )(page_table, lfc, lfst, lower)


def _sample_scores(b, q_row, bias_ref, kst_ref, k_refs, n_heads):
    d = q_row.shape[-1]
    n = kst_ref.shape[-1]
    own = (lax.broadcasted_iota(jnp.int32, (n_heads, d), 1) // HEAD_DIM
           == lax.broadcasted_iota(jnp.int32, (n_heads, d), 0))
    qbd = jnp.where(own, jnp.broadcast_to(q_row, (n_heads, d)), 0.0).astype(BF16)
    s = jnp.concatenate(
        [jnp.dot(qbd, k_ref[0].astype(BF16), preferred_element_type=F32) for k_ref in k_refs],
        axis=1) + bias_ref[0]
    is_b = lax.broadcasted_iota(jnp.int32, (n_heads, n), 1) == b
    k_self = jnp.where(lax.broadcasted_iota(jnp.int32, (d, n), 1) == b, kst_ref[...], 0.0).astype(BF16)
    s_self = jnp.where(is_b, jnp.dot(qbd, k_self, preferred_element_type=F32), NEG_BIG)
    m = jnp.maximum(jnp.max(s, axis=1, keepdims=True), jnp.max(s_self, axis=1, keepdims=True))
    p = jnp.exp(s - m)
    p_self = jnp.exp(s_self - m)
    l = jnp.sum(p, axis=1, keepdims=True) + jnp.sum(p_self, axis=1, keepdims=True)
    return p, p_self, l


def _weighted_values(p_pages, v_pages, h):
    rows = slice(h * HEAD_DIM, (h + 1) * HEAD_DIM)
    acc = None
    for pw, v in zip(p_pages, v_pages):
        term = v[rows, :] * pw[h:h + 1, :]
        acc = term if acc is None else acc + term
    return jnp.sum(acc, axis=1, keepdims=True)


def _ab_body(pt_ref, xs_ref, gb_ref, winb_ref, qg_ref, woutb_ref, bias_ref, kst_ref, vst_ref,
             xp_ref, ga_ref, wina_ref, cw_ref, cb_ref, lg_ref, lb_ref, wouta_ref, *rest,
             n_heads, n_pages, tiles_per_seq):
    k_refs = rest[:n_pages]
    v_refs = rest[n_pages:2 * n_pages]
    ys_ref, yp_ref, st_ref, q_sc, szs_sc, ot_sc, hist_ref, sza_ref, yc_ref = rest[2 * n_pages:]
    n, d = xs_ref.shape
    tt = xp_ref.shape[1]
    i = pl.program_id(0)
    t = i % tiles_per_seq

    @pl.when(i == 0)
    def _():
        xn = _rms_rows(xs_ref[...], gb_ref[...]).astype(BF16)
        u = jnp.dot(xn, winb_ref[0], preferred_element_type=F32)
        scale = 1.0 / math.sqrt(HEAD_DIM)
        q_sc[...] = (_head_rms_t(u[:, :d].T, qg_ref[...], n_heads) * scale).T
        szs_sc[...] = _silu(u[:, d:])
        ot_sc[...] = jnp.zeros_like(ot_sc)

    x = xp_ref[0]
    _a_tile_project(x, t == 0, ga_ref, wina_ref, hist_ref, sza_ref)
    p, p_self, l = _sample_scores(i, q_sc[pl.ds(i, 1), :], bias_ref, kst_ref, k_refs, n_heads)
    for piece in range((tt // CONV_ROWS) * (d // LANES)):
        _a_tile_conv_piece(piece, hist_ref, cw_ref, yc_ref)
    page = p_self.shape[-1]
    p_pages = [p_self] + [p[:, j * page:(j + 1) * page] for j in range(n_pages)]
    v_pages = [vst_ref] + [v_ref.at[0] for v_ref in v_refs]
    inv_l = 1.0 / l
    is_i = lax.broadcasted_iota(jnp.int32, (HEAD_DIM, n), 1) == i
    for h in range(n_heads):
        rows = slice(h * HEAD_DIM, (h + 1) * HEAD_DIM)
        oh = _weighted_values(p_pages, v_pages, h) * inv_l[h:h + 1, :]
        ot_sc[rows, :] = jnp.where(is_i, oh, ot_sc[rows, :])
    yp_ref[0] = _a_tile_finish(x, cb_ref, lg_ref, lb_ref, wouta_ref, sza_ref, yc_ref)

    @pl.when(t == tiles_per_seq - 1)
    def _():
        st_ref[0] = hist_ref[tt + CONV_HALO - (CONV_WIDTH - 1):tt + CONV_HALO, :]

    hist_ref[0:CONV_HALO, :] = hist_ref[tt:tt + CONV_HALO, :]

    @pl.when(i == pl.num_programs(0) - 1)
    def _():
        mm = (ot_sc[...].T * szs_sc[...]).astype(BF16)
        ys_ref[...] = xs_ref[...] + jnp.dot(mm, woutb_ref[0], preferred_element_type=F32)


def _ab_layer(layer, page_table, xs, gb, w_in_b, qg_col, w_out_b, bias, kst, vst, ckt, cvt,
              xp, ga, w_in_a, cw, cb, lg, lb, w_out_a):
    n, d = xs.shape
    bp, tp, _ = xp.shape
    n_pages = page_table.shape[1]
    n_heads = bias.shape[1]
    page = ckt.shape[-1]
    tt = A_TILE
    tiles_per_seq = tp // tt
    assert page == n, "the new token's key rides through the kernel as one more page-sized block"
    assert bp * tiles_per_seq == n, "one prompt tile per sample grid step"
    page_spec = lambda j: pl.BlockSpec((1, d, page), lambda i, pt, j=j: (pt[i, j], 0, 0))
    tile_spec = pl.BlockSpec((1, tt, d), lambda i, pt: (i // tiles_per_seq, i % tiles_per_seq, 0))
    row = lambda: _const_spec((1, d))
    return pl.pallas_call(
        functools.partial(_ab_body, n_heads=n_heads, n_pages=n_pages, tiles_per_seq=tiles_per_seq),
        grid_spec=pltpu.PrefetchScalarGridSpec(
            num_scalar_prefetch=1,
            grid=(n,),
            in_specs=[
                _const_spec((n, d)), row(), _layer_spec(layer, (d, 2 * d)), _const_spec((HEAD_DIM, 1)),
                _layer_spec(layer, (d, d)),
                pl.BlockSpec((1, n_heads, n_pages * page), lambda i, pt: (i, 0, 0)),
                _const_spec((d, n)), _const_spec((d, n)),
                tile_spec, row(), _layer_spec(layer, (d, 3 * d)), _const_spec((CONV_WIDTH, d)), row(), row(), row(),
                _layer_spec(layer, (d, d)),
            ] + [page_spec(j) for j in range(n_pages)] * 2,
            out_specs=[
                pl.BlockSpec((n, d), lambda i, pt: (0, 0)),
                tile_spec,
                pl.BlockSpec((1, CONV_WIDTH - 1, d), lambda i, pt: (i // tiles_per_seq, 0, 0)),
            ],
            scratch_shapes=[pltpu.VMEM((n, d), F32)] * 2 + [pltpu.VMEM((d, n), F32)] + [
                pltpu.VMEM((tt + CONV_HALO, d), F32), pltpu.VMEM((tt, d), F32), pltpu.VMEM((tt, d), F32)],
        ),
        out_shape=[
            jax.ShapeDtypeStruct((n, d), F32),
            jax.ShapeDtypeStruct((bp, tp, d), F32),
            jax.ShapeDtypeStruct((bp, CONV_WIDTH - 1, d), F32),
        ],
        compiler_params=_params("arbitrary"),
        name="ab_layer",
    )(page_table, xs, gb, w_in_b, qg_col, w_out_b, bias, kst, vst,
      xp, ga, w_in_a, cw, cb, lg, lb, w_out_a, *([ckt] * n_pages), *([cvt] * n_pages))


def kernel(x_prompt, x_sample, state_conv, cache_k, cache_v, cache_logf, page_table, a_norm, a_w_in, a_conv_w,
           a_conv_b, a_ln_g, a_ln_b, a_w_out, kv_norm, kv_w, kv_fb, k_norm, b_norm, b_w_in, q_norm, b_w_out):
    bp, tp, d = x_prompt.shape
    bs = x_sample.shape[0]
    n_a = a_w_in.shape[0]
    n_b = b_w_in.shape[0]
    n_heads = kv_fb.shape[0]
    attn = n_heads * HEAD_DIM
    n_pool, page = cache_k.shape[0], cache_k.shape[1]

    a_w_in_b = a_w_in.astype(BF16)
    a_w_out_b = a_w_out.astype(BF16)
    b_w_in_b = b_w_in.astype(BF16)
    b_w_out_b = b_w_out.astype(BF16)
    wk = kv_w[:, :attn].astype(BF16)
    wv = kv_w[:, attn:2 * attn].astype(BF16)
    wft = kv_w[:, 2 * attn:].T.astype(BF16)
    fb_col = kv_fb.reshape(n_heads, 1)
    kg_col = k_norm.reshape(HEAD_DIM, 1)
    kvg = kv_norm.reshape(1, d)

    st_t = jnp.transpose(state_conv, (0, 2, 1, 3))
    nst_t, xs, kst, vst, lfst = _a_sample(
        x_sample.reshape(bs, d), st_t, a_conv_w, a_norm[:, None], a_w_in_b, a_conv_b[:, None],
        a_ln_g[:, None], a_ln_b[:, None], a_w_out_b, kvg, wk, wv, wft, fb_col, kg_col)
    ckt = jnp.transpose(cache_k, (0, 2, 3, 1)).reshape(n_pool, attn, page)
    cvt = jnp.transpose(cache_v, (0, 2, 3, 1)).reshape(n_pool, attn, page)
    lfc = jnp.transpose(cache_logf, (0, 2, 1))
    bias = _bias_sample(page_table, lfc, lfst)

    assert n_a == n_b
    xp = x_prompt
    conv_p = []
    for j in range(n_b):
        xs, xp, st = _ab_layer(
            j, page_table, xs, b_norm[j][None], b_w_in_b, q_norm[j].reshape(HEAD_DIM, 1), b_w_out_b,
            bias, kst, vst, ckt, cvt,
            xp, a_norm[j][None], a_w_in_b, a_conv_w[j], a_conv_b[j][None], a_ln_g[j][None], a_ln_b[j][None],
            a_w_out_b)
        conv_p.append(st)

    kt, vt, lft, kb, vtb, ct = _kv_prompt(xp, kvg, wk, wv, wft, fb_col, kg_col)
    for j in range(n_b):
        xp = _b_prompt(j, xp, b_norm[j][None], b_w_in_b, q_norm[j].reshape(HEAD_DIM, 1), b_w_out_b,
                       kb, vtb, ct)

    heads_last = lambda a: jnp.transpose(a.reshape(a.shape[0], n_heads, HEAD_DIM, a.shape[-1]), (0, 3, 1, 2))
    return (
        xp,
        xs.reshape(bs, 1, d),
        jnp.stack(conv_p, axis=0),
        jnp.transpose(nst_t, (0, 2, 1, 3)),
        heads_last(kt),
        heads_last(vt),
        jnp.transpose(lft, (0, 2, 1)),
        jnp.transpose(kst.reshape(n_heads, HEAD_DIM, bs), (2, 0, 1))[:, None],
        jnp.transpose(vst.reshape(n_heads, HEAD_DIM, bs), (2, 0, 1))[:, None],
        jnp.transpose(lfst, (1, 0))[:, None],
    )
```
